```python
import jax, jax.numpy as jnp
from jax import lax
import numpy as np

D_MODEL = 2048
BATCH = 8
SEQ = 2048
DEPTH = 2
DEC_BATCH = 32
DEC_SEQ = 8
PAST_LEN = 8192
PAGE_SIZE = 128

N_EVEN = (DEPTH + 1) // 2
N_ODD = DEPTH // 2

NSA_HEADS = 16
NSA_HEAD_DIM = 128
NSA_KV_GROUPS = 4
NSA_REP = NSA_HEADS // NSA_KV_GROUPS
NSA_BLOCK = 64
NSA_TOP_BLOCKS = 16
NSA_WINDOW = 512
NSA_Q_BLOCK = 32
NSA_Q_WIDTH = NSA_HEADS * NSA_HEAD_DIM
NSA_KV_WIDTH = NSA_KV_GROUPS * NSA_HEAD_DIM
NSA_IN_WIDTH = NSA_Q_WIDTH + 6 * NSA_KV_WIDTH + 3 * NSA_HEADS
ROPE_THETA = 10000.0
FORCE_SCORE = 1.0e4
NEG_INF = -1.0e30

MLSTM_HEADS = 8
MLSTM_QK_DIM = 128
MLSTM_V_DIM = 256
MLSTM_CHUNK = 64
MLSTM_IN_WIDTH = 2 * MLSTM_HEADS * MLSTM_QK_DIM + 2 * MLSTM_HEADS * MLSTM_V_DIM + 2 * MLSTM_HEADS

D_FF = 5504
N_EXPERTS = 8
TOP_K = 2
D_FF_EXPERT = 7168
MOE_ROW_BLOCK = 128
NORM_EPS = 1e-6

kernel_name = "nsa_mlstm_hybrid_decode_step"

F32 = jnp.float32


def rmsnorm(x, g):
    xf = x.astype(F32)
    y = xf * lax.rsqrt(jnp.mean(xf * xf, axis=-1, keepdims=True) + NORM_EPS)
    return (y * g.astype(F32)).astype(x.dtype)


def adaln(c, w, b):
    mod = jax.nn.silu(c) @ w + b
    return jnp.split(mod[:, None, :], 6, axis=-1)


def modulate(x, g, shift, scale):
    return rmsnorm(x, g) * (1 + scale) + shift


def rope(x, pos):
    half = x.shape[-1] // 2
    freq = ROPE_THETA ** (-jnp.arange(half, dtype=F32) / half)
    ang = pos.astype(F32)[:, None] * freq[None, :]
    cos, sin = jnp.cos(ang)[:, None, :], jnp.sin(ang)[:, None, :]
    xf = x.astype(F32)
    x1, x2 = xf[..., :half], xf[..., half:]
    return jnp.concatenate([x1 * cos - x2 * sin, x1 * sin + x2 * cos], axis=-1).astype(x.dtype)


def swiglu(h, wg, wu, wd):
    return (jax.nn.silu(h @ wg) * (h @ wu)) @ wd


def nsa_project(h, w_in, pos):
    B, T, _ = h.shape
    G, R, HD = NSA_KV_GROUPS, NSA_REP, NSA_HEAD_DIM
    proj = h @ w_in
    q = rope(proj[..., :NSA_Q_WIDTH].reshape(B, T, NSA_HEADS, HD), pos).reshape(B, T, G, R, HD)
    kv = proj[..., NSA_Q_WIDTH:NSA_Q_WIDTH + 6 * NSA_KV_WIDTH].reshape(B, T, 3, 2, G, HD)
    k = rope(kv[:, :, :, 0].reshape(B, T, 3 * G, HD), pos).reshape(B, T, 3, G, HD)
    kv = jnp.stack([k, kv[:, :, :, 1]], axis=3)
    gates = jax.nn.sigmoid(proj[..., NSA_Q_WIDTH + 6 * NSA_KV_WIDTH:].astype(F32)).reshape(B, T, G, R, 3)
    return q, kv[:, :, 0], kv[:, :, 1], kv[:, :, 2], gates


def nsa_compress(rows, w_pool):
    B, L = rows.shape[:2]
    n = L // NSA_BLOCK
    blocks = rows[:, :n * NSA_BLOCK].reshape(B, n, NSA_BLOCK, *rows.shape[2:])
    return jnp.einsum('bnlsgd,sld->bnsgd', blocks, w_pool.astype(rows.dtype))


def gather_block_rows(rows, blk):
    B, L = rows.shape[:2]
    tok = jnp.clip(blk[..., None] * NSA_BLOCK + jnp.arange(NSA_BLOCK), 0, L - 1)
    bi = jnp.arange(B)[:, None, None, None, None]
    gi = jnp.arange(NSA_KV_GROUPS)[None, None, :, None, None]
    return rows[bi, tok, :, gi, :]


def nsa_attend(q, qpos, gates, ckv, n_blocks, gather_sel, wkv, wpos):
    B, T, G, R = q.shape[:4]
    scale = NSA_HEAD_DIM ** -0.5
    nc = ckv.shape[1]
    s_c = jnp.einsum('btgrd,bngd->btgrn', q, ckv[:, :, 0]).astype(F32) * scale
    c_ok = (((jnp.arange(nc) + 1) * NSA_BLOCK - 1)[None, :] <= qpos[:, None])[None, :, None, None, :]
    p_c = jax.nn.softmax(jnp.where(c_ok, s_c, NEG_INF), axis=-1) * c_ok
    o_c = jnp.einsum('btgrn,bngd->btgrd', p_c.astype(ckv.dtype), ckv[:, :, 1])
    imp = jnp.pad(p_c.sum(axis=3), ((0, 0), (0, 0), (0, 0), (0, n_blocks - nc)))
    blk = jnp.arange(n_blocks)[None, :]
    cur = (qpos // NSA_BLOCK)[:, None]
    forced = ((blk == 0) | (blk == cur) | (blk == cur - 1))[None, :, None, :]
    avail = (blk <= cur)[None, :, None, :]
    score = jnp.where(forced, FORCE_SCORE, jnp.where(avail, imp, -1.0))
    _, sel = lax.top_k(score, min(NSA_TOP_BLOCKS, n_blocks))
    skv = gather_sel(sel)
    s_s = jnp.einsum('btgrd,btgsld->btgrsl', q, skv[..., 0, :]).astype(F32) * scale
    tok = sel[..., None] * NSA_BLOCK + jnp.arange(NSA_BLOCK)
    s_ok = (tok <= qpos[None, :, None, None, None])[:, :, :, None]
    s_s = jnp.where(s_ok, s_s, NEG_INF)
    p_s = jax.nn.softmax(s_s.reshape(B, T, G, R, -1), axis=-1).reshape(s_s.shape)
    o_s = jnp.einsum('btgrsl,btgsld->btgrd', p_s.astype(skv.dtype), skv[..., 1, :])
    s_w = jnp.einsum('btgrd,bwgd->btgrw', q, wkv[:, :, 0]).astype(F32) * scale
    dist = qpos[:, None] - wpos[None, :]
    w_ok = ((wpos[None, :] >= 0) & (dist >= 0) & (dist < NSA_WINDOW))[None, :, None, None, :]
    p_w = jax.nn.softmax(jnp.where(w_ok, s_w, NEG_INF), axis=-1)
    o_w = jnp.einsum('btgrw,bwgd->btgrd', p_w.astype(wkv.dtype), wkv[:, :, 1])
    return gates[..., 0:1] * o_c + gates[..., 1:2] * o_s + gates[..., 2:3] * o_w


def nsa_prompt(h, w_in, w_pool, w_out, pos):
    B, T, _ = h.shape
    q, kv_c, kv_s, kv_w, gates = nsa_project(h, w_in, pos)
    ckv = nsa_compress(kv_c, w_pool)
    n_blocks = -(-T // NSA_BLOCK)
    wpad = jnp.pad(kv_w, ((0, 0), (NSA_WINDOW, 0), (0, 0), (0, 0), (0, 0)))
    span = NSA_WINDOW + NSA_Q_BLOCK

    def q_block(i):
        s = i * NSA_Q_BLOCK
        return nsa_attend(lax.dynamic_slice_in_dim(q, s, NSA_Q_BLOCK, 1),
                          s + jnp.arange(NSA_Q_BLOCK),
                          lax.dynamic_slice_in_dim(gates, s, NSA_Q_BLOCK, 1),
                          ckv, n_blocks, lambda sel: gather_block_rows(kv_s, sel),
                          lax.dynamic_slice_in_dim(wpad, s, span, 1),
                          s - NSA_WINDOW + jnp.arange(span))

    o = lax.map(q_block, jnp.arange(T // NSA_Q_BLOCK))
    o = jnp.moveaxis(o, 0, 1).reshape(B, T, NSA_Q_WIDTH).astype(h.dtype)
    return o @ w_out, kv_c, kv_s, kv_w[:, -min(NSA_WINDOW, T):]


def nsa_sample(h, w_in, w_pool, w_out, pos, cache_cmp, cache_sel, layer, page_table, win_kv):
    B, T, _ = h.shape
    G, HD = NSA_KV_GROUPS, NSA_HEAD_DIM
    past = page_table.shape[1] * PAGE_SIZE
    q, kv_c, kv_s, kv_w, gates = nsa_project(h, w_in, pos)
    past_c = cache_cmp[layer, page_table].reshape(B, past, 2, G, HD)
    ckv = jnp.concatenate([nsa_compress(past_c, w_pool), nsa_compress(kv_c, w_pool).astype(past_c.dtype)], axis=1)
    nbp = past // NSA_BLOCK
    n_blocks = nbp + -(-T // NSA_BLOCK)
    bpp = PAGE_SIZE // NSA_BLOCK

    def gather_sel(sel):
        lp = jnp.clip(sel, 0, nbp - 1)
        phys = page_table[jnp.arange(B)[:, None, None, None], lp // bpp]
        rows = ((lp % bpp) * NSA_BLOCK)[..., None] + jnp.arange(NSA_BLOCK)
        gi = jnp.arange(G)[None, None, :, None, None]
        from_past = cache_sel[layer, phys[..., None], rows, :, gi, :]
        from_new = gather_block_rows(kv_s, sel - nbp).astype(from_past.dtype)
        return jnp.where((sel < nbp)[..., None, None, None], from_past, from_new)

    wb = win_kv.shape[1]
    wkv = jnp.concatenate([win_kv, kv_w.astype(win_kv.dtype)], axis=1)
    wpos = past - wb + jnp.arange(wb + T)
    o = nsa_attend(q, pos, gates, ckv, n_blocks, gather_sel, wkv, wpos)
    o = o.reshape(B, T, NSA_Q_WIDTH).astype(h.dtype)
    return o @ w_out, kv_c, kv_s, wkv[:, -wb:]


def mlstm_project(h, w_in, b_gate):
    B, T, _ = h.shape
    H, DK, DV = MLSTM_HEADS, MLSTM_QK_DIM, MLSTM_V_DIM
    proj = h @ w_in
    o1, o2 = H * DK, 2 * H * DK
    o3, o4 = o2 + H * DV, o2 + 2 * H * DV
    q = proj[..., :o1].reshape(B, T, H, DK).astype(F32)
    k = proj[..., o1:o2].reshape(B, T, H, DK).astype(F32) * (DK ** -0.5)
    v = proj[..., o2:o3].reshape(B, T, H, DV).astype(F32)
    o = proj[..., o3:o4]
    g = proj[..., o4:].astype(F32) + b_gate.astype(F32)
    return q, k, v, o, g[..., :H], jax.nn.log_sigmoid(g[..., H:])


def mlstm_chunk(state, xs):
    C, n, m = state
    q, k, v, ig, lf = xs
    L = q.shape[1]
    b = jnp.cumsum(lf, axis=1)
    causal = jnp.tril(jnp.ones((L, L), bool))[None, :, :, None]
    d = jnp.where(causal, b[:, :, None, :] - b[:, None, :, :] + ig[:, None, :, :], -jnp.inf)
    inter = b + m[:, None, :]
    m_t = jnp.maximum(inter, d.max(axis=2))
    w = jnp.exp(d - m_t[:, :, None, :]) * jnp.einsum('bthd,bshd->btsh', q, k)
    a = jnp.exp(inter - m_t)
    num = a[..., None] * jnp.einsum('bthd,bhde->bthe', q, C) + jnp.einsum('btsh,bshe->bthe', w, v)
    den = a * jnp.einsum('bthd,bhd->bth', q, n) + w.sum(axis=2)
    h = num / jnp.maximum(jnp.abs(den), jnp.exp(-m_t))[..., None]
    m_new = m_t[:, -1]
    ws = jnp.exp(b[:, -1:, :] - b + ig - m_new[:, None, :])
    decay = jnp.exp(b[:, -1] + m - m_new)
    kw = k * ws[..., None]
    C_new = decay[..., None, None] * C + jnp.einsum('bshd,bshe->bhde', kw, v)
    n_new = decay[..., None] * n + kw.sum(axis=1)
    return (C_new, n_new, m_new), h


def mlstm_output(hs, o, norm_g, w_out, dtype):
    B, T = hs.shape[:2]
    hn = hs * lax.rsqrt(jnp.mean(hs * hs, axis=-1, keepdims=True) + NORM_EPS) * norm_g.astype(F32)
    y = hn * jax.nn.sigmoid(o.astype(F32)).reshape(hn.shape)
    return y.reshape(B, T, -1).astype(dtype) @ w_out


def mlstm_prompt(h, w_in, b_gate, norm_g, w_out):
    B, T, _ = h.shape
    H, DK, DV = MLSTM_HEADS, MLSTM_QK_DIM, MLSTM_V_DIM
    q, k, v, o, ig, lf = mlstm_project(h, w_in, b_gate)
    n_chunks = T // MLSTM_CHUNK
    to_chunks = lambda a: jnp.moveaxis(a.reshape(B, n_chunks, MLSTM_CHUNK, *a.shape[2:]), 1, 0)
    init = (jnp.zeros((B, H, DK, DV), F32), jnp.zeros((B, H, DK), F32), jnp.zeros((B, H), F32))
    (C, n, m), hs = lax.scan(mlstm_chunk, init, (to_chunks(q), to_chunks(k), to_chunks(v), to_chunks(ig), to_chunks(lf)))
    hs = jnp.moveaxis(hs, 0, 1).reshape(B, T, H, DV)
    dt = h.dtype
    return mlstm_output(hs, o, norm_g, w_out, dt), C.astype(dt), n.astype(dt), m.astype(dt)


def mlstm_sample(h, w_in, b_gate, norm_g, w_out, C0, n0, m0):
    q, k, v, o, ig, lf = mlstm_project(h, w_in, b_gate)
    (C, n, m), hs = mlstm_chunk((C0.astype(F32), n0.astype(F32), m0.astype(F32)), (q, k, v, ig, lf))
    return mlstm_output(hs, o, norm_g, w_out, h.dtype), C.astype(C0.dtype), n.astype(n0.dtype), m.astype(m0.dtype)


def moe_swiglu(h, w_router, wg, wu, wd):
    B, T, Dm = h.shape
    N = B * T
    xf = h.reshape(N, Dm)
    logits = (xf @ w_router).astype(F32)
    top_v, top_e = lax.top_k(logits, TOP_K)
    gate = jax.nn.softmax(top_v, axis=-1)
    A = N * TOP_K
    flat_e = top_e.reshape(A)
    flat_t = jnp.arange(A) // TOP_K
    order = jnp.argsort(flat_e * A + jnp.arange(A))
    se = flat_e[order]
    counts = jnp.bincount(flat_e, length=N_EXPERTS)
    padded = (counts + MOE_ROW_BLOCK - 1) // MOE_ROW_BLOCK * MOE_ROW_BLOCK
    starts = jnp.cumsum(counts) - counts
    p_end = jnp.cumsum(padded)
    p_start = p_end - padded
    dest = p_start[se] + jnp.arange(A) - starts[se]
    n_rows = -(-(A + N_EXPERTS * (MOE_ROW_BLOCK - 1)) // MOE_ROW_BLOCK) * MOE_ROW_BLOCK
    row_tok = jnp.full((n_rows,), N, jnp.int32).at[dest].set(flat_t[order])
    blk_e = jnp.minimum(jnp.searchsorted(p_end, jnp.arange(n_rows // MOE_ROW_BLOCK) * MOE_ROW_BLOCK, side='right'), N_EXPERTS - 1)
    xb = jnp.concatenate([xf, jnp.zeros((1, Dm), xf.dtype)], axis=0)[row_tok].reshape(-1, MOE_ROW_BLOCK, Dm)
    yb = lax.map(lambda a: swiglu(a[0], wg[a[1]], wu[a[1]], wd[a[1]]), (xb, blk_e))
    y_sorted = yb.reshape(n_rows, Dm)[dest]
    y_assign = jnp.zeros_like(y_sorted).at[order].set(y_sorted)
    y = jnp.einsum('nkd,nk->nd', y_assign.reshape(N, TOP_K, Dm), gate.astype(y_assign.dtype))
    return y.reshape(B, T, Dm)


def setup_inputs(seed: int = 0) -> dict:
    key = jax.random.key(seed)
    ks = jax.random.split(key, 40)
    nrm = lambda i, shape, s: jax.random.normal(ks[i], shape, F32) * s
    D, G, HD = D_MODEL, NSA_KV_GROUPS, NSA_HEAD_DIM
    H, DK, DV = MLSTM_HEADS, MLSTM_QK_DIM, MLSTM_V_DIM
    n_pages = PAST_LEN // PAGE_SIZE
    n_used = DEC_BATCH * n_pages
    n_phys = n_used + max(1, n_used // 4)
    win = min(NSA_WINDOW, PAST_LEN)
    page_table = jax.random.permutation(ks[6], n_phys)[:n_used].reshape(DEC_BATCH, n_pages).astype(jnp.int32)
    b_gate = jnp.concatenate([nrm(17, (N_ODD, H), 0.1),
                              jax.random.uniform(ks[18], (N_ODD, H), F32, 3.0, 6.0)], axis=-1)
    return {
        "x_prompt": nrm(0, (BATCH, SEQ, D), 1.0),
        "x_sample": nrm(1, (DEC_BATCH, DEC_SEQ, D), 1.0),
        "c_prompt": nrm(2, (BATCH, D), 1.0),
        "c_sample": nrm(3, (DEC_BATCH, D), 1.0),
        "cache_nsa_cmp_kv": nrm(4, (N_EVEN, n_phys, PAGE_SIZE, 2, G, HD), 1.0),
        "cache_nsa_sel_kv": nrm(5, (N_EVEN, n_phys, PAGE_SIZE, 2, G, HD), 1.0),
        "page_table": page_table,
        "state_nsa_win_kv": nrm(7, (N_EVEN, DEC_BATCH, win, 2, G, HD), 1.0),
        "state_mlstm_c": nrm(8, (N_ODD, DEC_BATCH, H, DK, DV), 0.05),
        "state_mlstm_n": nrm(9, (N_ODD, DEC_BATCH, H, DK), 0.05),
        "state_mlstm_m": nrm(10, (N_ODD, DEC_BATCH, H), 0.5),
        "ada_w": nrm(11, (DEPTH, D, 6 * D), 0.5 * D ** -0.5),
        "ada_b": nrm(12, (DEPTH, 6 * D), 0.02),
        "norm_mix_g": 1.0 + nrm(13, (DEPTH, D), 0.02),
        "norm_ffn_g": 1.0 + nrm(14, (DEPTH, D), 0.02),
        "norm_final_g": 1.0 + nrm(15, (D,), 0.02),
        "nsa_w_in": nrm(16, (N_EVEN, D, NSA_IN_WIDTH), D ** -0.5),
        "nsa_w_pool": (1.0 + nrm(19, (N_EVEN, 2, NSA_BLOCK, HD), 0.1)) / NSA_BLOCK,
        "nsa_w_out": nrm(20, (N_EVEN, NSA_Q_WIDTH, D), NSA_Q_WIDTH ** -0.5),
        "mlstm_w_in": nrm(21, (N_ODD, D, MLSTM_IN_WIDTH), D ** -0.5),
        "mlstm_b_gate": b_gate,
        "mlstm_norm_g": 1.0 + nrm(22, (N_ODD, H, DV), 0.02),
        "mlstm_w_out": nrm(23, (N_ODD, H * DV, D), (H * DV) ** -0.5),
        "ffn_w_gate": nrm(24, (N_EVEN, D, D_FF), D ** -0.5),
        "ffn_w_up": nrm(25, (N_EVEN, D, D_FF), D ** -0.5),
        "ffn_w_down": nrm(26, (N_EVEN, D_FF, D), D_FF ** -0.5),
        "moe_w_router": nrm(27, (N_ODD, D, N_EXPERTS), D ** -0.5),
        "moe_w_gate": nrm(28, (N_ODD, N_EXPERTS, D, D_FF_EXPERT), D ** -0.5),
        "moe_w_up": nrm(29, (N_ODD, N_EXPERTS, D, D_FF_EXPERT), D ** -0.5),
        "moe_w_down": nrm(30, (N_ODD, N_EXPERTS, D_FF_EXPERT, D), D_FF_EXPERT ** -0.5),
    }


def reference(x_prompt, x_sample, c_prompt, c_sample, cache_nsa_cmp_kv, cache_nsa_sel_kv, page_table,
              state_nsa_win_kv, state_mlstm_c, state_mlstm_n, state_mlstm_m, ada_w, ada_b,
              norm_mix_g, norm_ffn_g, norm_final_g, nsa_w_in, nsa_w_pool, nsa_w_out,
              mlstm_w_in, mlstm_b_gate, mlstm_norm_g, mlstm_w_out, ffn_w_gate, ffn_w_up, ffn_w_down,
              moe_w_router, moe_w_gate, moe_w_up, moe_w_down):
    pos_p = jnp.arange(x_prompt.shape[1])
    pos_s = page_table.shape[1] * PAGE_SIZE + jnp.arange(x_sample.shape[1])
    xp, xs = x_prompt, x_sample
    cmp_p, sel_p, win_p, cmp_s, sel_s, win_s = [], [], [], [], [], []
    mc_p, mn_p, mm_p, mc_s, mn_s, mm_s = [], [], [], [], [], []
    for layer in range(DEPTH):
        j = layer // 2
        sh1p, sc1p, g1p, sh2p, sc2p, g2p = adaln(c_prompt, ada_w[layer], ada_b[layer])
        sh1s, sc1s, g1s, sh2s, sc2s, g2s = adaln(c_sample, ada_w[layer], ada_b[layer])
        hp = modulate(xp, norm_mix_g[layer], sh1p, sc1p)
        hs = modulate(xs, norm_mix_g[layer], sh1s, sc1s)
        if layer % 2 == 0:
            yp, kc, ksl, kw = nsa_prompt(hp, nsa_w_in[j], nsa_w_pool[j], nsa_w_out[j], pos_p)
            cmp_p.append(kc); sel_p.append(ksl); win_p.append(kw)
            ys, kc, ksl, kw = nsa_sample(hs, nsa_w_in[j], nsa_w_pool[j], nsa_w_out[j], pos_s,
                                         cache_nsa_cmp_kv, cache_nsa_sel_kv, j, page_table, state_nsa_win_kv[j])
            cmp_s.append(kc); sel_s.append(ksl); win_s.append(kw)
        else:
            yp, C, n, m = mlstm_prompt(hp, mlstm_w_in[j], mlstm_b_gate[j], mlstm_norm_g[j], mlstm_w_out[j])
            mc_p.append(C); mn_p.append(n); mm_p.append(m)
            ys, C, n, m = mlstm_sample(hs, mlstm_w_in[j], mlstm_b_gate[j], mlstm_norm_g[j], mlstm_w_out[j],
                                       state_mlstm_c[j], state_mlstm_n[j], state_mlstm_m[j])
            mc_s.append(C); mn_s.append(n); mm_s.append(m)
        xp = xp + g1p * yp
        xs = xs + g1s * ys
        hp = modulate(xp, norm_ffn_g[layer], sh2p, sc2p)
        hs = modulate(xs, norm_ffn_g[layer], sh2s, sc2s)
        if layer % 2 == 0:
            fp = swiglu(hp, ffn_w_gate[j], ffn_w_up[j], ffn_w_down[j])
            fs = swiglu(hs, ffn_w_gate[j], ffn_w_up[j], ffn_w_down[j])
        else:
            fp = moe_swiglu(hp, moe_w_router[j], moe_w_gate[j], moe_w_up[j], moe_w_down[j])
            fs = moe_swiglu(hs, moe_w_router[j], moe_w_gate[j], moe_w_up[j], moe_w_down[j])
        xp = xp + g2p * fp
        xs = xs + g2s * fs
    y_prompt = rmsnorm(xp, norm_final_g)
    y_sample = rmsnorm(xs, norm_final_g)
    return (y_prompt, y_sample,
            jnp.stack(cmp_p), jnp.stack(sel_p), jnp.stack(win_p),
            jnp.stack(mc_p), jnp.stack(mn_p), jnp.stack(mm_p),
            jnp.stack(cmp_s), jnp.stack(sel_s), jnp.stack(win_s),
            jnp.stack(mc_s), jnp.stack(mn_s), jnp.stack(mm_s))
```

```python
import functools

import jax
import jax.numpy as jnp
from jax import lax
from jax.experimental import pallas as pl
from jax.experimental.pallas import tpu as pltpu

F32 = jnp.float32
BF16 = jnp.bfloat16

PAGE_SIZE = 128
NSA_HEADS = 16
NSA_HEAD_DIM = 128
NSA_KV_GROUPS = 4
NSA_REP = NSA_HEADS // NSA_KV_GROUPS
NSA_BLOCK = 64
NSA_TOP_BLOCKS = 16
NSA_WINDOW = 512
NSA_Q_WIDTH = NSA_HEADS * NSA_HEAD_DIM
NSA_KV_WIDTH = NSA_KV_GROUPS * NSA_HEAD_DIM
ROPE_THETA = 10000.0
FORCE_SCORE = 1.0e4
NEG_INF = -1.0e30
MLSTM_HEADS = 8
MLSTM_QK_DIM = 128
MLSTM_V_DIM = 256
MLSTM_CHUNK = 64
N_EXPERTS = 8
TOP_K = 2
NORM_EPS = 1e-6

LANES = 128
SUBLANES = 8
VMEM_LIMIT_BYTES = 52 * 1024 * 1024

_NT = (((1,), (1,)), ((), ()))
_TN = (((0,), (0,)), ((), ()))


def _params(*sem):
    return pltpu.CompilerParams(dimension_semantics=sem, vmem_limit_bytes=VMEM_LIMIT_BYTES)


def _dot(a, b):
    return jnp.dot(a.astype(BF16), b.astype(BF16), preferred_element_type=F32)


def _dot_nt(a, b):
    return lax.dot_general(a.astype(BF16), b.astype(BF16), _NT, preferred_element_type=F32)


def _sigmoid(x):
    return 1.0 / (1.0 + jnp.exp(-x))


def _silu(x):
    return x * _sigmoid(x)


def _adaln_kernel(c_ref, w_ref, b_ref, o_ref):
    o_ref[...] = _dot(_silu(c_ref[...]), w_ref[...]) + b_ref[...]


def _adaln(c_all, ada_w, ada_b):
    depth, d, n = ada_w.shape
    tn = _row_tile(n, 1024)
    rows = c_all.shape[0]
    return pl.pallas_call(
        _adaln_kernel,
        grid=(depth, n // tn),
        in_specs=[pl.BlockSpec((rows, d), lambda l, j: (0, 0)),
                  pl.BlockSpec((None, d, tn), lambda l, j: (l, 0, j)),
                  pl.BlockSpec((None, 1, tn), lambda l, j: (l, 0, j))],
        out_specs=pl.BlockSpec((None, rows, tn), lambda l, j: (l, 0, j)),
        out_shape=jax.ShapeDtypeStruct((depth, rows, n), F32),
        compiler_params=_params("arbitrary", "arbitrary"),
        name="adaln",
    )(c_all, ada_w, ada_b.reshape(depth, 1, n))


def _rms(x):
    return x * lax.rsqrt(jnp.mean(x * x, axis=-1, keepdims=True) + NORM_EPS)


def _modulate_kernel(x_ref, g_ref, shift_ref, scale_ref, o_ref):
    y = _rms(x_ref[...]) * g_ref[...]
    o_ref[...] = (y * (1.0 + scale_ref[...]) + shift_ref[...]).astype(o_ref.dtype)


def _mod_spec(mod, kind, chunk, width, tm, rows_per_batch):
    d6 = mod.shape[-1]
    per_chunk = (d6 // 6) // width
    if kind == "batch":
        return pl.BlockSpec((None, 1, width),
                            lambda i, j: ((i * tm) // rows_per_batch, 0, chunk * per_chunk + j))
    return pl.BlockSpec((tm, width), lambda i, j: (i, chunk * per_chunk + j))


def _modulate(x, g, mod, kind, shift_chunk, scale_chunk, tm, rows_per_batch):
    m, d = x.shape
    return pl.pallas_call(
        _modulate_kernel,
        grid=(m // tm, 1),
        in_specs=[pl.BlockSpec((tm, d), lambda i, j: (i, 0)),
                  pl.BlockSpec((1, d), lambda i, j: (0, 0)),
                  _mod_spec(mod, kind, shift_chunk, d, tm, rows_per_batch),
                  _mod_spec(mod, kind, scale_chunk, d, tm, rows_per_batch)],
        out_specs=pl.BlockSpec((tm, d), lambda i, j: (i, 0)),
        out_shape=jax.ShapeDtypeStruct((m, d), BF16),
        compiler_params=_params("arbitrary", "arbitrary"),
        name="modulate",
    )(x, g.reshape(1, d), mod, mod)


def _mm_kernel(x_ref, w_ref, o_ref, *, act):
    acc = _dot(x_ref[...], w_ref[...])
    if act == "sigmoid":
        acc = _sigmoid(acc)
    o_ref[...] = acc.astype(o_ref.dtype)


def _mm(x, w, *, n_out=None, tm, tn, act=None, out_dtype=F32, name="mm"):
    m, k = x.shape
    n = w.shape[1] if n_out is None else n_out
    return pl.pallas_call(
        functools.partial(_mm_kernel, act=act),
        grid=(n // tn, m // tm),
        in_specs=[pl.BlockSpec((tm, k), lambda j, i: (i, 0)),
                  pl.BlockSpec((k, tn), lambda j, i: (0, j))],
        out_specs=pl.BlockSpec((tm, tn), lambda j, i: (i, j)),
        out_shape=jax.ShapeDtypeStruct((m, n), out_dtype),
        compiler_params=_params("arbitrary", "arbitrary"),
        name=name,
    )(x, w)


def _rope_tiles(width):
    q_tiles = NSA_Q_WIDTH // width
    kv_tiles = NSA_KV_WIDTH // width
    return q_tiles, kv_tiles


def _mm_rope_kernel(x_ref, w_ref, cos_ref, sin_ref, o_ref, *, tn):
    j = pl.program_id(0)
    acc = _dot(x_ref[...], w_ref[...])
    q_tiles, kv_tiles = _rope_tiles(tn)
    jk = j - q_tiles
    is_rope = (j < q_tiles) | ((jk // kv_tiles) % 2 == 0)

    @pl.when(is_rope)
    def _():
        cos = cos_ref[...]
        sin = sin_ref[...]
        for s in range(tn // NSA_HEAD_DIM):
            sl = slice(s * NSA_HEAD_DIM, (s + 1) * NSA_HEAD_DIM)
            a = acc[:, sl]
            o_ref[:, sl] = a * cos + pltpu.roll(a, NSA_HEAD_DIM // 2, 1) * sin

    @pl.when(jnp.logical_not(is_rope))
    def _():
        o_ref[...] = acc


def _mm_rope(x, w, cos, sin, *, n_out, tm, tn):
    m, k = x.shape
    pos_tiles = cos.shape[0] // tm
    return pl.pallas_call(
        functools.partial(_mm_rope_kernel, tn=tn),
        grid=(n_out // tn, m // tm),
        in_specs=[pl.BlockSpec((tm, k), lambda j, i: (i, 0)),
                  pl.BlockSpec((k, tn), lambda j, i: (0, j)),
                  pl.BlockSpec((tm, NSA_HEAD_DIM), lambda j, i: (i % pos_tiles, 0)),
                  pl.BlockSpec((tm, NSA_HEAD_DIM), lambda j, i: (i % pos_tiles, 0))],
        out_specs=pl.BlockSpec((tm, tn), lambda j, i: (i, j)),
        out_shape=jax.ShapeDtypeStruct((m, n_out), F32),
        compiler_params=_params("arbitrary", "arbitrary"),
        name="nsa_in_proj",
    )(x, w, cos, sin)


def _mm_res_kernel(a_ref, w_ref, res_ref, gate_ref, o_ref):
    o_ref[...] = res_ref[...] + gate_ref[...] * _dot(a_ref[...], w_ref[...])


def _mm_res(a, w, res, mod, kind, gate_chunk, *, tm, tn, rows_per_batch):
    m, k = a.shape
    n = w.shape[1]
    swap = lambda f: (lambda j, i: f(i, j))
    gspec = _mod_spec(mod, kind, gate_chunk, tn, tm, rows_per_batch)
    gspec = pl.BlockSpec(gspec.block_shape, swap(gspec.index_map))
    return pl.pallas_call(
        _mm_res_kernel,
        grid=(n // tn, m // tm),
        in_specs=[pl.BlockSpec((tm, k), lambda j, i: (i, 0)),
                  pl.BlockSpec((k, tn), lambda j, i: (0, j)),
                  pl.BlockSpec((tm, tn), lambda j, i: (i, j)),
                  gspec],
        out_specs=pl.BlockSpec((tm, tn), lambda j, i: (i, j)),
        out_shape=jax.ShapeDtypeStruct((m, n), F32),
        compiler_params=_params("arbitrary", "arbitrary"),
        name="mm_residual",
    )(a, w, res, mod)


def _swiglu_up_kernel(x_ref, wg_ref, wu_ref, o_ref):
    x = x_ref[...]
    o_ref[...] = (_silu(_dot(x, wg_ref[...])) * _dot(x, wu_ref[...])).astype(o_ref.dtype)


def _swiglu_up(x, wg, wu, *, tm, tn):
    m, k = x.shape
    f = wg.shape[1]
    return pl.pallas_call(
        _swiglu_up_kernel,
        grid=(pl.cdiv(f, tn), m // tm),
        in_specs=[pl.BlockSpec((tm, k), lambda j, i: (i, 0)),
                  pl.BlockSpec((k, tn), lambda j, i: (0, j)),
                  pl.BlockSpec((k, tn), lambda j, i: (0, j))],
        out_specs=pl.BlockSpec((tm, tn), lambda j, i: (i, j)),
        out_shape=jax.ShapeDtypeStruct((m, f), BF16),
        compiler_params=_params("arbitrary", "arbitrary"),
        name="swiglu_up",
    )(x, wg, wu)


def _grouped_scalar_kernel(e_ref, *refs, body):
    del e_ref
    body(*refs)


def _moe_up(xb, blk_e, wg, wu, *, tm, tn):
    rows, k = xb.shape
    f = wg.shape[2]
    grid_spec = pltpu.PrefetchScalarGridSpec(
        num_scalar_prefetch=1,
        grid=(f // tn, rows // tm),
        in_specs=[pl.BlockSpec((tm, k), lambda j, r, e: (r, 0)),
                  pl.BlockSpec((None, k, tn), lambda j, r, e: (e[r], 0, j)),
                  pl.BlockSpec((None, k, tn), lambda j, r, e: (e[r], 0, j))],
        out_specs=pl.BlockSpec((tm, tn), lambda j, r, e: (r, j)))
    return pl.pallas_call(
        functools.partial(_grouped_scalar_kernel, body=_swiglu_up_kernel),
        grid_spec=grid_spec,
        out_shape=jax.ShapeDtypeStruct((rows, f), BF16),
        compiler_params=_params("arbitrary", "arbitrary"),
        name="moe_up",
    )(blk_e, xb, wg, wu)


def _mm_plain_kernel(a_ref, w_ref, o_ref):
    o_ref[...] = _dot(a_ref[...], w_ref[...])


def _moe_down(act, blk_e, wd, *, tm, tn):
    rows, f = act.shape
    d = wd.shape[2]
    grid_spec = pltpu.PrefetchScalarGridSpec(
        num_scalar_prefetch=1,
        grid=(d // tn, rows // tm),
        in_specs=[pl.BlockSpec((tm, f), lambda j, r, e: (r, 0)),
                  pl.BlockSpec((None, f, tn), lambda j, r, e: (e[r], 0, j))],
        out_specs=pl.BlockSpec((tm, tn), lambda j, r, e: (r, j)))
    return pl.pallas_call(
        functools.partial(_grouped_scalar_kernel, body=_mm_plain_kernel),
        grid_spec=grid_spec,
        out_shape=jax.ShapeDtypeStruct((rows, d), F32),
        compiler_params=_params("arbitrary", "arbitrary"),
        name="moe_down",
    )(blk_e, act, wd)


def _router_kernel(x_ref, w_ref, o_ref):
    lg = _dot(x_ref[...], w_ref[...])
    lane = lax.broadcasted_iota(jnp.int32, lg.shape, 1)
    x = jnp.where(lane < N_EXPERTS, lg, -jnp.inf)
    v0 = jnp.max(x, axis=-1, keepdims=True)
    i0 = jnp.min(jnp.where(x == v0, lane, LANES), axis=-1, keepdims=True)
    x1 = jnp.where(lane == i0, -jnp.inf, x)
    v1 = jnp.max(x1, axis=-1, keepdims=True)
    i1 = jnp.min(jnp.where(x1 == v1, lane, LANES), axis=-1, keepdims=True)
    e1 = jnp.exp(v1 - v0)
    g0 = 1.0 / (1.0 + e1)
    g1 = e1 / (1.0 + e1)
    o_ref[...] = jnp.where(lane == 0, g0,
                           jnp.where(lane == 1, g1,
                                     jnp.where(lane == 2, i0.astype(F32),
                                               jnp.where(lane == 3, i1.astype(F32), 0.0))))


def _router(h, w_router, tm):
    n, d = h.shape
    w_pad = jnp.pad(w_router, ((0, 0), (0, LANES - N_EXPERTS)))
    out = pl.pallas_call(
        _router_kernel,
        grid=(n // tm,),
        in_specs=[pl.BlockSpec((tm, d), lambda i: (i, 0)),
                  pl.BlockSpec((d, LANES), lambda i: (0, 0))],
        out_specs=pl.BlockSpec((tm, LANES), lambda i: (i, 0)),
        out_shape=jax.ShapeDtypeStruct((n, LANES), F32),
        compiler_params=_params("arbitrary"),
        name="moe_router",
    )(h, w_pad)
    return out[:, :TOP_K], out[:, TOP_K:2 * TOP_K].astype(jnp.int32)


def _combine_norm_kernel(x_ref, y0_ref, y1_ref, w_ref, gate_ref, g_ref, o_ref):
    w = w_ref[...]
    y = w[:, 0:1] * y0_ref[...] + w[:, 1:2] * y1_ref[...]
    xo = x_ref[...] + gate_ref[...] * y
    o_ref[...] = _rms(xo) * g_ref[...]


def _combine_norm(x, y0, y1, wts, mod, kind, gate_chunk, g, *, tm, rows_per_batch):
    m, d = x.shape
    row = pl.BlockSpec((tm, d), lambda i, j: (i, 0))
    return pl.pallas_call(
        _combine_norm_kernel,
        grid=(m // tm, 1),
        in_specs=[row, row, row,
                  pl.BlockSpec((tm, TOP_K), lambda i, j: (i, 0)),
                  _mod_spec(mod, kind, gate_chunk, d, tm, rows_per_batch),
                  pl.BlockSpec((1, d), lambda i, j: (0, 0))],
        out_specs=row,
        out_shape=jax.ShapeDtypeStruct((m, d), F32),
        compiler_params=_params("arbitrary", "arbitrary"),
        name="moe_combine_final_norm",
    )(x, y0, y1, wts, mod, g.reshape(1, d))


def _compress_kernel(*refs, n_in):
    w = refs[n_in][...]
    o_ref = refs[n_in + 1]
    outs = []
    for x_ref in refs[:n_in]:
        x = x_ref[...]
        nb = x.shape[0] // NSA_BLOCK
        outs.append(jnp.sum(x.reshape(nb, NSA_BLOCK, x.shape[1]) * w[None], axis=1))
    o_ref[...] = outs[0] if n_in == 1 else jnp.concatenate(outs, axis=0)


def _pool_weights(w_pool):
    w = jnp.transpose(w_pool, (1, 0, 2))[:, :, None, :]
    w = jnp.broadcast_to(w, (NSA_BLOCK, 2, NSA_KV_GROUPS, NSA_HEAD_DIM))
    return w.reshape(NSA_BLOCK, 2 * NSA_KV_WIDTH)


def _compress_prompt(proj, w_full, rows_per_step):
    b, t, _ = proj.shape
    kvw = 2 * NSA_KV_WIDTH
    col_blk = NSA_Q_WIDTH // kvw
    nb = rows_per_step // NSA_BLOCK
    return pl.pallas_call(
        functools.partial(_compress_kernel, n_in=1),
        grid=(b, t // rows_per_step),
        in_specs=[pl.BlockSpec((None, rows_per_step, kvw), lambda i, s: (i, s, col_blk)),
                  pl.BlockSpec((NSA_BLOCK, kvw), lambda i, s: (0, 0))],
        out_specs=pl.BlockSpec((None, nb, kvw), lambda i, s: (i, s, 0)),
        out_shape=jax.ShapeDtypeStruct((b, t // NSA_BLOCK, kvw), F32),
        compiler_params=_params("arbitrary", "arbitrary"),
        name="nsa_compress_prompt",
    )(proj, w_full)


def _compress_pages(cache, page_table, w_full, pages_per_step):
    bsz, n_pages = page_table.shape
    kvw = cache.shape[-1]
    p = pages_per_step
    bpp = PAGE_SIZE // NSA_BLOCK

    def page_spec(k):
        return pl.BlockSpec((None, PAGE_SIZE, kvw), lambda i, s, pt: (pt[i, s * p + k], 0, 0))

    grid_spec = pltpu.PrefetchScalarGridSpec(
        num_scalar_prefetch=1,
        grid=(bsz, n_pages // p),
        in_specs=[page_spec(k) for k in range(p)] + [pl.BlockSpec((NSA_BLOCK, kvw), lambda i, s, pt: (0, 0))],
        out_specs=pl.BlockSpec((None, p * bpp, kvw), lambda i, s, pt: (i, s, 0)))

    def body(pt_ref, *refs):
        del pt_ref
        _compress_kernel(*refs, n_in=p)

    return pl.pallas_call(
        body,
        grid_spec=grid_spec,
        out_shape=jax.ShapeDtypeStruct((bsz, n_pages * bpp, kvw), F32),
        compiler_params=_params("arbitrary", "arbitrary"),
        name="nsa_compress_pages",
    )(page_table, *([cache] * p), w_full)


def _select_blocks(score, n_valid):
    rows, width = score.shape
    col = lax.broadcasted_iota(jnp.int32, (rows, width), 1)
    rank = jnp.zeros((rows, width), F32)
    for m in range(n_valid):
        c = score[:, m:m + 1]
        beats = jnp.where(c > score, 1.0, jnp.where((c == score) & (col > m), 1.0, 0.0))
        rank = rank + beats
    return jnp.where((rank < NSA_TOP_BLOCKS) & (col < n_valid), 1.0, 0.0)


def _softmax_rows(s):
    e = jnp.exp(s - jnp.max(s, axis=-1, keepdims=True))
    return e, jnp.sum(e, axis=-1, keepdims=True)


def _compressed_branch(q4, ck, cv, tpos, reps):
    scale = NSA_HEAD_DIM ** -0.5
    nc = ck.shape[0]
    tq = q4.shape[0] // reps
    s = _dot_nt(q4, ck) * scale
    blk = lax.broadcasted_iota(jnp.int32, (1, nc), 1)
    ok = ((blk + 1) * NSA_BLOCK - 1) <= tpos
    e, l = _softmax_rows(jnp.where(ok, s, NEG_INF))
    p = jnp.where(ok, e / l, 0.0)
    o_c = _dot(p, cv)
    imp = p[0:tq]
    for r in range(1, reps):
        imp = imp + p[r * tq:(r + 1) * tq]
    return o_c, imp


def _block_scores(imp, qpos, n_blocks_pad):
    tq, n_imp = imp.shape
    if n_blocks_pad > n_imp:
        imp = jnp.concatenate([imp, jnp.zeros((tq, n_blocks_pad - n_imp), F32)], axis=1)
    blk = lax.broadcasted_iota(jnp.int32, (1, n_blocks_pad), 1)
    cur = qpos // NSA_BLOCK
    forced = (blk == 0) | (blk == cur) | (blk == cur - 1)
    return jnp.where(forced, FORCE_SCORE, jnp.where(blk <= cur, imp, -1.0))


def _nsa_prompt_kernel(q_ref, ck_ref, cv_ref, ks_ref, vs_ref, kw_ref, vw_ref, gate_ref, o_ref, *, tq, kc, seq):
    hd, reps = NSA_HEAD_DIM, NSA_REP
    scale = hd ** -0.5
    q0 = pl.program_id(2) * tq
    qf = q_ref[...]
    q4 = jnp.concatenate([qf[:, r * hd:(r + 1) * hd] for r in range(reps)], axis=0).astype(BF16)
    qpos = q0 + lax.broadcasted_iota(jnp.int32, (tq, 1), 0)
    tpos = jnp.concatenate([qpos] * reps, axis=0)
    m_rows = reps * tq

    n_blocks = seq // NSA_BLOCK
    o_c, imp = _compressed_branch(q4, ck_ref[...], cv_ref[...], tpos, reps)
    sel = _select_blocks(_block_scores(imp, qpos, n_blocks), n_blocks).astype(BF16)

    blk_row = lax.broadcasted_iota(jnp.int32, (n_blocks, 1), 0)

    def chunk(c, carry):
        m_i, l_i, acc = carry
        k0 = pl.multiple_of(c * kc, kc)
        kk = ks_ref[pl.ds(k0, kc), :]
        vv = vs_ref[pl.ds(k0, kc), :]
        kpos = k0 + lax.broadcasted_iota(jnp.int32, (1, kc), 1)
        expand = jnp.where(kpos // NSA_BLOCK == blk_row, 1.0, 0.0).astype(BF16)
        selk = jnp.dot(sel, expand, preferred_element_type=F32)
        selk = jnp.concatenate([selk] * reps, axis=0)
        s = _dot_nt(q4, kk) * scale
        s = jnp.where(kpos <= tpos, jnp.where(selk > 0.5, s, NEG_INF), NEG_INF)
        m_new = jnp.maximum(m_i, jnp.max(s, axis=-1, keepdims=True))
        alpha = jnp.exp(m_i - m_new)
        p = jnp.exp(s - m_new)
        l_new = alpha * l_i + jnp.sum(p, axis=-1, keepdims=True)
        return m_new, l_new, alpha * acc + _dot(p, vv)

    n_chunks = (q0 + tq + kc - 1) // kc
    init = (jnp.full((m_rows, 1), NEG_INF, F32), jnp.zeros((m_rows, 1), F32), jnp.zeros((m_rows, hd), F32))
    _, l_s, acc_s = lax.fori_loop(0, n_chunks, chunk, init)
    o_s = acc_s / l_s

    span = NSA_WINDOW + tq
    w0 = pl.multiple_of(jnp.maximum(q0 - NSA_WINDOW, 0), tq)
    kpos = w0 + lax.broadcasted_iota(jnp.int32, (1, span), 1)
    dist = tpos - kpos
    s = _dot_nt(q4, kw_ref[pl.ds(w0, span), :]) * scale
    s = jnp.where(dist >= 0, jnp.where(dist < NSA_WINDOW, s, NEG_INF), NEG_INF)
    e, l = _softmax_rows(s)
    o_w = _dot(e, vw_ref[pl.ds(w0, span), :]) / l

    gate = gate_ref[...]
    for r in range(reps):
        rows = slice(r * tq, (r + 1) * tq)
        o = (gate[:, 3 * r:3 * r + 1] * o_c[rows] + gate[:, 3 * r + 1:3 * r + 2] * o_s[rows]
             + gate[:, 3 * r + 2:3 * r + 3] * o_w[rows])
        o_ref[:, r * hd:(r + 1) * hd] = o.astype(o_ref.dtype)


def _nsa_prompt_attention(proj, ckv, gates, *, tq, kc):
    b, t, _ = proj.shape
    g_, hd = NSA_KV_GROUPS, NSA_HEAD_DIM
    qw = NSA_REP * hd
    nc = ckv.shape[1]
    kcol = lambda branch, kv: (NSA_Q_WIDTH + branch * 2 * NSA_KV_WIDTH + kv * NSA_KV_WIDTH) // hd
    rows = lambda off: pl.BlockSpec((None, t, hd), lambda i, g, s: (i, 0, off + g))
    return pl.pallas_call(
        functools.partial(_nsa_prompt_kernel, tq=tq, kc=kc, seq=t),
        grid=(b, g_, t // tq),
        in_specs=[pl.BlockSpec((None, tq, qw), lambda i, g, s: (i, s, g)),
                  pl.BlockSpec((None, nc, hd), lambda i, g, s: (i, 0, g)),
                  pl.BlockSpec((None, nc, hd), lambda i, g, s: (i, 0, g_ + g)),
                  rows(kcol(1, 0)), rows(kcol(1, 1)), rows(kcol(2, 0)), rows(kcol(2, 1)),
                  pl.BlockSpec((None, tq, LANES), lambda i, g, s: (i, s, g))],
        out_specs=pl.BlockSpec((None, tq, qw), lambda i, g, s: (i, s, g)),
        out_shape=jax.ShapeDtypeStruct((b, t, NSA_Q_WIDTH), BF16),
        compiler_params=_params("arbitrary", "arbitrary", "arbitrary"),
        name="nsa_prompt_attention",
    )(proj, ckv, ckv, proj, proj, proj, proj, gates)


def _nsa_sample_select_kernel(q_ref, ckv_ref, oc_ref, sel_ref, *, t_new, past, n_blocks, n_pad):
    hd, reps = NSA_HEAD_DIM, NSA_REP
    qpos = past + lax.broadcasted_iota(jnp.int32, (t_new, 1), 0)
    tpos = jnp.concatenate([qpos] * reps, axis=0)
    for g in range(NSA_KV_GROUPS):
        ck = ckv_ref[:, g * hd:(g + 1) * hd]
        cv = ckv_ref[:, NSA_KV_WIDTH + g * hd:NSA_KV_WIDTH + (g + 1) * hd]
        o_c, imp = _compressed_branch(q_ref[g].astype(BF16), ck, cv, tpos, reps)
        oc_ref[g] = o_c
        sel_ref[g] = _select_blocks(_block_scores(imp, qpos, n_pad), n_blocks)


def _nsa_sample_select(q4, ckv, *, t_new, past, n_blocks, n_pad):
    b = q4.shape[0]
    g_, hd = NSA_KV_GROUPS, NSA_HEAD_DIM
    rows = NSA_REP * t_new
    nc = ckv.shape[1]
    return pl.pallas_call(
        functools.partial(_nsa_sample_select_kernel, t_new=t_new, past=past, n_blocks=n_blocks, n_pad=n_pad),
        grid=(b,),
        in_specs=[pl.BlockSpec((None, g_, rows, hd), lambda i: (i, 0, 0, 0)),
                  pl.BlockSpec((None, nc, 2 * NSA_KV_WIDTH), lambda i: (i, 0, 0))],
        out_specs=[pl.BlockSpec((None, g_, rows, hd), lambda i: (i, 0, 0, 0)),
                   pl.BlockSpec((None, g_, t_new, n_pad), lambda i: (i, 0, 0, 0))],
        out_shape=[jax.ShapeDtypeStruct((b, g_, rows, hd), F32),
                   jax.ShapeDtypeStruct((b, g_, t_new, n_pad), F32)],
        compiler_params=_params("arbitrary"),
        name="nsa_sample_select",
    )(q4, ckv)


def _nsa_sample_attend_kernel(pt_ref, *refs, n_pg, t_new, past, n_pad):
    del pt_ref
    pages = refs[:n_pg]
    (q_ref, sel_ref, oc_ref, new_s_ref, win_ref, new_w_ref, gate_ref,
     o_ref, win_out_ref, m_ref, l_ref, acc_ref) = refs[n_pg:]
    hd, reps, g_ = NSA_HEAD_DIM, NSA_REP, NSA_KV_GROUPS
    kvw = NSA_KV_WIDTH
    scale = hd ** -0.5
    step = pl.program_id(1)
    rows = reps * t_new
    qpos = past + lax.broadcasted_iota(jnp.int32, (t_new, 1), 0)
    tpos = jnp.concatenate([qpos] * reps, axis=0)
    nbp = past // NSA_BLOCK

    def online(g, s, vv):
        m_i = m_ref[g]
        m_new = jnp.maximum(m_i, jnp.max(s, axis=-1, keepdims=True))
        alpha = jnp.exp(m_i - m_new)
        p = jnp.exp(s - m_new)
        l_ref[g] = alpha * l_ref[g] + jnp.sum(p, axis=-1, keepdims=True)
        acc_ref[g] = alpha * acc_ref[g] + _dot(p, vv)
        m_ref[g] = m_new

    def masked(sel_g, first_blk, width, kpos, s):
        blk_row = lax.broadcasted_iota(jnp.int32, (n_pad, 1), 0)
        key_blk = first_blk + lax.broadcasted_iota(jnp.int32, (1, width), 1) // NSA_BLOCK
        expand = jnp.where(key_blk == blk_row, 1.0, 0.0).astype(BF16)
        selk = jnp.dot(sel_g.astype(BF16), expand, preferred_element_type=F32)
        selk = jnp.concatenate([selk] * reps, axis=0)
        return jnp.where(kpos <= tpos, jnp.where(selk > 0.5, s, NEG_INF), NEG_INF)

    def new_rows(ref, c0):
        x = ref[:, c0:c0 + hd]
        return jnp.concatenate([x, jnp.zeros((PAGE_SIZE - t_new, hd), x.dtype)], axis=0)

    @pl.when(step == 0)
    def _():
        m_ref[...] = jnp.full(m_ref.shape, NEG_INF, F32)
        l_ref[...] = jnp.zeros(l_ref.shape, F32)
        acc_ref[...] = jnp.zeros(acc_ref.shape, F32)
        kpos = past + lax.broadcasted_iota(jnp.int32, (1, PAGE_SIZE), 1)
        for g in range(g_):
            kk = new_rows(new_s_ref, g * hd)
            vv = new_rows(new_s_ref, kvw + g * hd)
            s = _dot_nt(q_ref[g], kk) * scale
            online(g, masked(sel_ref[g], nbp, PAGE_SIZE, kpos, s), vv)

    width = n_pg * PAGE_SIZE
    first_key = step * width
    kpos = first_key + lax.broadcasted_iota(jnp.int32, (1, width), 1)
    for g in range(g_):
        kk = jnp.concatenate([pg[:, g * hd:(g + 1) * hd].astype(BF16) for pg in pages], axis=0)
        vv = jnp.concatenate([pg[:, kvw + g * hd:kvw + (g + 1) * hd].astype(BF16) for pg in pages], axis=0)
        s = _dot_nt(q_ref[g], kk) * scale
        online(g, masked(sel_ref[g], first_key // NSA_BLOCK, width, kpos, s), vv)

    @pl.when(step == pl.num_programs(1) - 1)
    def _():
        wb = win_ref.shape[0]
        wpos = past - wb + lax.broadcasted_iota(jnp.int32, (1, wb + PAGE_SIZE), 1)
        dist = tpos - wpos
        gate = gate_ref[...]
        for g in range(g_):
            kk = jnp.concatenate([win_ref[:, g * hd:(g + 1) * hd], new_rows(new_w_ref, g * hd)], axis=0)
            vv = jnp.concatenate([win_ref[:, kvw + g * hd:kvw + (g + 1) * hd],
                                  new_rows(new_w_ref, kvw + g * hd)], axis=0)
            s = _dot_nt(q_ref[g], kk) * scale
            ok = (wpos >= 0) & (dist >= 0) & (dist < NSA_WINDOW)
            e, l = _softmax_rows(jnp.where(ok, s, NEG_INF))
            o_w = _dot(e, vv) / l
            o_s = acc_ref[g] / l_ref[g]
            o_c = oc_ref[g]
            for r in range(reps):
                rs = slice(r * t_new, (r + 1) * t_new)
                c0 = g * LANES + 3 * r
                o = (gate[:, c0:c0 + 1] * o_c[rs] + gate[:, c0 + 1:c0 + 2] * o_s[rs]
                     + gate[:, c0 + 2:c0 + 3] * o_w[rs])
                col = (g * reps + r) * hd
                o_ref[:, col:col + hd] = o.astype(o_ref.dtype)
        win_out_ref[0:wb - t_new, :] = win_ref[t_new:wb, :]
        win_out_ref[wb - t_new:wb, :] = new_w_ref[...]


def _nsa_sample_attend(q4, sel, o_c, proj, win, gates, cache, page_table, *, n_pg, past):
    b, g_, rows, hd = q4.shape
    t_new = rows // NSA_REP
    n_pad = sel.shape[-1]
    kvw2 = 2 * NSA_KV_WIDTH
    n_pages = page_table.shape[1]
    wb = win.shape[1]

    def page_spec(k):
        return pl.BlockSpec((None, PAGE_SIZE, kvw2), lambda i, s, pt: (pt[i, s * n_pg + k], 0, 0))

    per_b = lambda shape: pl.BlockSpec((None,) + shape, lambda i, s, pt: (i,) + (0,) * len(shape))
    sel_col = NSA_Q_WIDTH // kvw2 + 1
    win_col = NSA_Q_WIDTH // kvw2 + 2
    grid_spec = pltpu.PrefetchScalarGridSpec(
        num_scalar_prefetch=1,
        grid=(b, n_pages // n_pg),
        in_specs=[page_spec(k) for k in range(n_pg)] + [
            per_b((g_, rows, hd)), per_b((g_, t_new, n_pad)), per_b((g_, rows, hd)),
            pl.BlockSpec((None, t_new, kvw2), lambda i, s, pt: (i, 0, sel_col)),
            per_b((wb, kvw2)),
            pl.BlockSpec((None, t_new, kvw2), lambda i, s, pt: (i, 0, win_col)),
            per_b((t_new, g_ * LANES))],
        out_specs=[per_b((t_new, NSA_Q_WIDTH)), per_b((wb, kvw2))],
        scratch_shapes=[pltpu.VMEM((g_, rows, 1), F32), pltpu.VMEM((g_, rows, 1), F32),
                        pltpu.VMEM((g_, rows, hd), F32)])
    return pl.pallas_call(
        functools.partial(_nsa_sample_attend_kernel, n_pg=n_pg, t_new=t_new, past=past, n_pad=n_pad),
        grid_spec=grid_spec,
        out_shape=[jax.ShapeDtypeStruct((b, t_new, NSA_Q_WIDTH), BF16),
                   jax.ShapeDtypeStruct((b, wb, kvw2), F32)],
        compiler_params=_params("arbitrary", "arbitrary"),
        name="nsa_sample_attention",
    )(page_table, *([cache] * n_pg), q4.astype(BF16), sel, o_c, proj, win, proj, gates)


def _log_sigmoid(x):
    return jnp.minimum(x, 0.0) - jnp.log1p(jnp.exp(-jnp.abs(x)))


def _mlstm_kernel(q_ref, k_ref, v_ref, o_ref, gc_ref, gr_ref, bc_ref, br_ref, ng_ref,
                  c0_ref, n0_ref, m0_ref, y_ref, c_out, n_out, m_out, c_s, n_s, m_s):
    nh, dk, dv = MLSTM_HEADS, MLSTM_QK_DIM, MLSTM_V_DIM
    step = pl.program_id(1)
    ln = q_ref.shape[0]

    @pl.when(step == 0)
    def _():
        c_s[...] = c0_ref[...]
        n_s[...] = n0_ref[...]
        m_s[...] = m0_ref[...]

    hi = lax.Precision.HIGHEST
    r_i = lax.broadcasted_iota(jnp.int32, (ln, ln), 0)
    c_i = lax.broadcasted_iota(jnp.int32, (ln, ln), 1)
    causal = c_i <= r_i
    lower = jnp.where(causal, 1.0, 0.0)
    upper = jnp.where(r_i <= c_i, 1.0, 0.0)
    gcol = gc_ref[...] + bc_ref[...]
    grow = gr_ref[...] + br_ref[...]
    ig_c = gcol[:, :nh]
    b_c = jnp.dot(lower, _log_sigmoid(gcol[:, nh:]), precision=hi, preferred_element_type=F32)
    ig_r = grow[:nh, :]
    b_r = jnp.dot(_log_sigmoid(grow[nh:, :]), upper, precision=hi, preferred_element_type=F32)

    for h in range(nh):
        q = q_ref[:, h * dk:(h + 1) * dk]
        k = k_ref[:, h * dk:(h + 1) * dk] * (dk ** -0.5)
        v = v_ref[:, h * dv:(h + 1) * dv]
        qb, kb, vb = q.astype(BF16), k.astype(BF16), v.astype(BF16)
        bt = b_c[:, h:h + 1]
        b_last = b_c[ln - 1:ln, h:h + 1]
        m_prev = m_s[h]
        d = jnp.where(causal, bt - b_r[h:h + 1, :] + ig_r[h:h + 1, :], -jnp.inf)
        inter = bt + m_prev
        m_t = jnp.maximum(inter, jnp.max(d, axis=-1, keepdims=True))
        w = jnp.exp(d - m_t) * _dot_nt(qb, kb)
        a = jnp.exp(inter - m_t)
        c_prev = c_s[h]
        n_prev = n_s[h]
        num = a * _dot(qb, c_prev) + _dot(w, vb)
        den = a * jnp.sum(q * n_prev, axis=-1, keepdims=True) + jnp.sum(w, axis=-1, keepdims=True)
        hh = num / jnp.maximum(jnp.abs(den), jnp.exp(-m_t))
        m_new = m_t[ln - 1:ln, :]
        ws = jnp.exp(b_last - bt + ig_c[:, h:h + 1] - m_new)
        decay = jnp.exp(b_last + m_prev - m_new)
        kw = k * ws
        c_s[h] = decay * c_prev + lax.dot_general(kw.astype(BF16), vb, _TN, preferred_element_type=F32)
        n_s[h] = decay * n_prev + jnp.sum(kw, axis=0, keepdims=True)
        m_s[h] = m_new
        hn = _rms(hh) * ng_ref[:, h * dv:(h + 1) * dv]
        y_ref[:, h * dv:(h + 1) * dv] = (hn * _sigmoid(o_ref[:, h * dv:(h + 1) * dv])).astype(y_ref.dtype)

    @pl.when(step == pl.num_programs(1) - 1)
    def _():
        c_out[...] = c_s[...]
        n_out[...] = n_s[...]
        m_out[...] = m_s[...]


def _mlstm(proj, graw, b_gate, norm_g, c0, n0, m0, *, chunk):
    b, t, _ = proj.shape
    nh, dk, dv = MLSTM_HEADS, MLSTM_QK_DIM, MLSTM_V_DIM
    nchunk = t // chunk
    qw, vw = nh * dk, nh * dv
    grow = jnp.transpose(graw.reshape(b, nchunk, chunk, 2 * nh), (0, 1, 3, 2))
    state = lambda shape: pl.BlockSpec((None,) + shape, lambda i, s: (i,) + (0,) * len(shape))
    const = lambda shape: pl.BlockSpec(shape, lambda i, s: (0,) * len(shape))
    return pl.pallas_call(
        _mlstm_kernel,
        grid=(b, nchunk),
        in_specs=[pl.BlockSpec((None, chunk, qw), lambda i, s: (i, s, 0)),
                  pl.BlockSpec((None, chunk, qw), lambda i, s: (i, s, 1)),
                  pl.BlockSpec((None, chunk, vw), lambda i, s: (i, s, 2 * qw // vw)),
                  pl.BlockSpec((None, chunk, vw), lambda i, s: (i, s, 2 * qw // vw + 1)),
                  pl.BlockSpec((None, chunk, 2 * nh), lambda i, s: (i, s, 0)),
                  pl.BlockSpec((None, None, 2 * nh, chunk), lambda i, s: (i, s, 0, 0)),
                  const((1, 2 * nh)), const((2 * nh, 1)), const((1, vw)),
                  state((nh, dk, dv)), state((nh, 1, dk)), state((nh, 1, 1))],
        out_specs=[pl.BlockSpec((None, chunk, vw), lambda i, s: (i, s, 0)),
                   state((nh, dk, dv)), state((nh, 1, dk)), state((nh, 1, 1))],
        out_shape=[jax.ShapeDtypeStruct((b, t, vw), BF16),
                   jax.ShapeDtypeStruct((b, nh, dk, dv), F32),
                   jax.ShapeDtypeStruct((b, nh, 1, dk), F32),
                   jax.ShapeDtypeStruct((b, nh, 1, 1), F32)],
        scratch_shapes=[pltpu.VMEM((nh, dk, dv), F32), pltpu.VMEM((nh, 1, dk), F32),
                        pltpu.VMEM((nh, 1, 1), F32)],
        compiler_params=_params("arbitrary", "arbitrary"),
        name="mlstm",
    )(proj, proj, proj, proj, graw, grow, b_gate.reshape(1, 2 * nh), b_gate.reshape(2 * nh, 1),
      norm_g.reshape(1, vw), c0, n0.reshape(b, nh, 1, dk), m0.reshape(b, nh, 1, 1))


def _rope_tables(pos):
    half = NSA_HEAD_DIM // 2
    freq = ROPE_THETA ** (-jnp.arange(half, dtype=F32) / half)
    ang = pos.astype(F32)[:, None] * freq[None, :]
    cos, sin = jnp.cos(ang), jnp.sin(ang)
    return jnp.concatenate([cos, cos], axis=-1), jnp.concatenate([-sin, sin], axis=-1)


def _gate_weights(w_in):
    d = w_in.shape[0]
    per_group = NSA_REP * 3
    w = w_in[:, NSA_Q_WIDTH + 6 * NSA_KV_WIDTH:].reshape(d, NSA_KV_GROUPS, per_group)
    w = jnp.pad(w, ((0, 0), (0, 0), (0, LANES - per_group)))
    return w.reshape(d, NSA_KV_GROUPS * LANES)


def _row_tile(m, pref):
    t = min(pref, m)
    while m % t:
        t //= 2
    return t


class _Group:
    def __init__(self, x, mod_rows, kind):
        self.b, self.t, self.d = x.shape
        self.x = x.reshape(self.b * self.t, self.d)
        self.kind = kind
        self.mod_rows = mod_rows

    def mod(self, layer):
        m = self.mod_rows[layer]
        if self.kind == "batch":
            return m.reshape(self.b, 1, m.shape[-1])
        return jnp.repeat(m, self.t, axis=0)


def _kv_out(proj, branch, b, t):
    c0 = NSA_Q_WIDTH + branch * 2 * NSA_KV_WIDTH
    return proj[:, :, c0:c0 + 2 * NSA_KV_WIDTH].reshape(b, t, 2, NSA_KV_GROUPS, NSA_HEAD_DIM)


def kernel(x_prompt, x_sample, c_prompt, c_sample, cache_nsa_cmp_kv, cache_nsa_sel_kv, page_table, state_nsa_win_kv, state_mlstm_c, state_mlstm_n, state_mlstm_m, ada_w, ada_b, norm_mix_g, norm_ffn_g, norm_final_g, nsa_w_in, nsa_w_pool, nsa_w_out, mlstm_w_in, mlstm_b_gate, mlstm_norm_g, mlstm_w_out, ffn_w_gate, ffn_w_up, ffn_w_down, moe_w_router, moe_w_gate, moe_w_up, moe_w_down):
    bp, tp, d = x_prompt.shape
    bs, ts, _ = x_sample.shape
    past = page_table.shape[1] * PAGE_SIZE
    assert tp % NSA_BLOCK == 0 and tp >= NSA_WINDOW + 128 and ts < NSA_BLOCK and ts % SUBLANES == 0
    assert past % NSA_BLOCK == 0 and state_nsa_win_kv.shape[2] == NSA_WINDOW
    assert ada_w.shape[0] == 2 and nsa_w_in.shape[0] == 1 and mlstm_w_in.shape[0] == 1

    mod_all = _adaln(jnp.concatenate([c_prompt, c_sample], axis=0), ada_w, ada_b)
    groups = [_Group(x_prompt, mod_all[:, :bp], "batch"), _Group(x_sample, mod_all[:, bp:], "row")]
    tms = [_row_tile(tp, 512), bs * ts]

    w_in = nsa_w_in[0]
    w_gates = _gate_weights(w_in)
    w_full = _pool_weights(nsa_w_pool[0])
    kvw2 = 2 * NSA_KV_WIDTH
    n_proj = NSA_Q_WIDTH + 3 * kvw2
    outs = {}
    for gi, (grp, tm) in enumerate(zip(groups, tms)):
        b, t = grp.b, grp.t
        mod = grp.mod(0)
        h = _modulate(grp.x, norm_mix_g[0], mod, grp.kind, 0, 1, min(tm, t) if grp.kind == "batch" else tm, t)
        pos = jnp.arange(t) if gi == 0 else past + jnp.arange(t)
        cos, sin = _rope_tables(pos)
        if gi == 1:
            cos, sin = jnp.tile(cos, (b, 1)), jnp.tile(sin, (b, 1))
        tmm = min(tm, t) if gi == 0 else tm
        proj = _mm_rope(h, w_in, cos, sin, n_out=n_proj, tm=tmm, tn=512).reshape(b, t, n_proj)
        gates = _mm(h, w_gates, tm=tmm, tn=NSA_KV_GROUPS * LANES, act="sigmoid", name="nsa_gates")
        if gi == 0:
            ckv = _compress_prompt(proj, w_full, 8 * NSA_BLOCK)
            att = _nsa_prompt_attention(proj, ckv, gates.reshape(b, t, -1), tq=128, kc=512)
            win_out = _kv_out(proj, 2, b, t)[:, -min(NSA_WINDOW, t):]
        else:
            n_phys = cache_nsa_cmp_kv.shape[1]
            ckv = _compress_pages(cache_nsa_cmp_kv[0].reshape(n_phys, PAGE_SIZE, kvw2), page_table, w_full, 4)
            n_blocks = past // NSA_BLOCK + 1
            n_pad = -(-n_blocks // LANES) * LANES
            q4 = proj[:, :, :NSA_Q_WIDTH].reshape(b, t, NSA_KV_GROUPS, NSA_REP, NSA_HEAD_DIM)
            q4 = jnp.transpose(q4, (0, 2, 3, 1, 4)).reshape(b, NSA_KV_GROUPS, NSA_REP * t, NSA_HEAD_DIM)
            o_c, sel = _nsa_sample_select(q4, ckv, t_new=t, past=past, n_blocks=n_blocks, n_pad=n_pad)
            att, win_new = _nsa_sample_attend(
                q4, sel, o_c, proj, state_nsa_win_kv[0].reshape(b, NSA_WINDOW, kvw2), gates.reshape(b, t, -1),
                cache_nsa_sel_kv[0].reshape(n_phys, PAGE_SIZE, kvw2), page_table, n_pg=8, past=past)
            win_out = win_new.reshape(b, NSA_WINDOW, 2, NSA_KV_GROUPS, NSA_HEAD_DIM)
        outs[gi] = (_kv_out(proj, 0, b, t)[None], _kv_out(proj, 1, b, t)[None], win_out[None])
        x1 = _mm_res(att.reshape(b * t, NSA_Q_WIDTH), nsa_w_out[0], grp.x, mod, grp.kind, 2,
                     tm=tmm, tn=256, rows_per_batch=t)
        h2 = _modulate(x1, norm_ffn_g[0], mod, grp.kind, 3, 4, tmm, t)
        act = _swiglu_up(h2, ffn_w_gate[0], ffn_w_up[0], tm=tmm, tn=512)
        grp.x = _mm_res(act, ffn_w_down[0], x1, mod, grp.kind, 5, tm=tmm, tn=256, rows_per_batch=t)

    nh, dk, dv = MLSTM_HEADS, MLSTM_QK_DIM, MLSTM_V_DIM
    n_main = 2 * nh * dk + 2 * nh * dv
    w_g = jnp.pad(mlstm_w_in[0][:, n_main:], ((0, 0), (0, LANES - 2 * nh)))
    states = {}
    hs2, mods = [], []
    for gi, (grp, tm) in enumerate(zip(groups, tms)):
        b, t = grp.b, grp.t
        mod = grp.mod(1)
        tmm = min(tm, t) if gi == 0 else tm
        h = _modulate(grp.x, norm_mix_g[1], mod, grp.kind, 0, 1, tmm, t)
        proj = _mm(h, mlstm_w_in[0], n_out=n_main, tm=tmm, tn=512, name="mlstm_in_proj").reshape(b, t, n_main)
        graw = _mm(h, w_g, tm=tmm, tn=LANES, name="mlstm_gates")[:, :2 * nh].reshape(b, t, 2 * nh)
        if gi == 0:
            c0 = jnp.zeros((b, nh, dk, dv), F32)
            n0 = jnp.zeros((b, nh, dk), F32)
            m0 = jnp.zeros((b, nh), F32)
            chunk = MLSTM_CHUNK
        else:
            c0, n0, m0 = state_mlstm_c[0], state_mlstm_n[0], state_mlstm_m[0]
            chunk = t
        y, c_new, n_new, m_new = _mlstm(proj, graw, mlstm_b_gate[0], mlstm_norm_g[0], c0, n0, m0, chunk=chunk)
        states[gi] = (c_new[None], n_new.reshape(1, b, nh, dk), m_new.reshape(1, b, nh))
        grp.x = _mm_res(y.reshape(b * t, nh * dv), mlstm_w_out[0], grp.x, mod, grp.kind, 2,
                        tm=tmm, tn=256, rows_per_batch=t)
        hs2.append(_modulate(grp.x, norm_ffn_g[1], mod, grp.kind, 3, 4, tmm, t))
        mods.append(mod)

    h_all = jnp.concatenate(hs2, axis=0)
    n_tok = h_all.shape[0]
    tm_moe = 512
    wts, top_e = _router(h_all, moe_w_router[0], _row_tile(n_tok, 256))
    n_asg = n_tok * TOP_K
    flat_e = top_e.reshape(n_asg)
    onehot = (flat_e[:, None] == jnp.arange(N_EXPERTS)[None, :]).astype(jnp.int32)
    within = jnp.take_along_axis(jnp.cumsum(onehot, axis=0), flat_e[:, None], axis=1)[:, 0] - 1
    counts = jnp.sum(onehot, axis=0)
    padded = (counts + tm_moe - 1) // tm_moe * tm_moe
    p_end = jnp.cumsum(padded)
    dest = (p_end - padded)[flat_e] + within
    n_blk = -(-(n_asg + N_EXPERTS * (tm_moe - 1)) // tm_moe)
    n_rows = n_blk * tm_moe
    row_tok = jnp.full((n_rows,), n_tok, jnp.int32).at[dest].set(jnp.arange(n_asg, dtype=jnp.int32) // TOP_K)
    blk_e = jnp.minimum(jnp.searchsorted(p_end, jnp.arange(n_blk) * tm_moe, side="right"),
                        N_EXPERTS - 1).astype(jnp.int32)
    xb = jnp.concatenate([h_all, jnp.zeros((1, d), h_all.dtype)], axis=0)[row_tok]
    act = _moe_up(xb, blk_e, moe_w_gate[0], moe_w_up[0], tm=tm_moe, tn=512)
    yb = _moe_down(act, blk_e, moe_w_down[0], tm=tm_moe, tn=256)
    dest2 = dest.reshape(n_tok, TOP_K)
    y0, y1 = yb[dest2[:, 0]], yb[dest2[:, 1]]

    finals = []
    off = 0
    for gi, (grp, tm) in enumerate(zip(groups, tms)):
        m = grp.b * grp.t
        tmm = min(tm, grp.t) if gi == 0 else tm
        fin = _combine_norm(grp.x, y0[off:off + m], y1[off:off + m], wts[off:off + m], mods[gi], grp.kind, 5,
                            norm_final_g, tm=tmm, rows_per_batch=grp.t)
        finals.append(fin.reshape(grp.b, grp.t, d))
        off += m

    return (finals[0], finals[1],
            outs[0][0], outs[0][1], outs[0][2], states[0][0], states[0][1], states[0][2],
            outs[1][0], outs[1][1], outs[1][2], states[1][0], states[1][1], states[1][2])
```

```python
import functools

import jax
import jax.numpy as jnp
from jax import lax
from jax.experimental import pallas as pl
from jax.experimental.pallas import tpu as pltpu

F32 = jnp.float32
BF16 = jnp.bfloat16

PAGE_SIZE = 128
NSA_HEADS = 16
NSA_HEAD_DIM = 128
NSA_KV_GROUPS = 4
NSA_REP = NSA_HEADS // NSA_KV_GROUPS
NSA_BLOCK = 64
NSA_TOP_BLOCKS = 16
NSA_WINDOW = 512
NSA_Q_WIDTH = NSA_HEADS * NSA_HEAD_DIM
NSA_KV_WIDTH = NSA_KV_GROUPS * NSA_HEAD_DIM
ROPE_THETA = 10000.0
FORCE_SCORE = 1.0e4
NEG_INF = -1.0e30
MLSTM_HEADS = 8
MLSTM_QK_DIM = 128
MLSTM_V_DIM = 256
MLSTM_CHUNK = 64
N_EXPERTS = 8
TOP_K = 2
MOE_ROW_BLOCK = 512
NORM_EPS = 1e-6
LOG2_E = 1.4426950408889634

LANES = 128
SUBLANES = 8
VMEM_LIMIT_BYTES = 52 * 1024 * 1024

_NT = (((1,), (1,)), ((), ()))
_TN = (((0,), (0,)), ((), ()))


def _params(*sem):
    return pltpu.CompilerParams(dimension_semantics=sem, vmem_limit_bytes=VMEM_LIMIT_BYTES)


def _dot(a, b):
    return jnp.dot(a.astype(BF16), b.astype(BF16), preferred_element_type=F32)


def _dot_nt(a, b):
    return lax.dot_general(a.astype(BF16), b.astype(BF16), _NT, preferred_element_type=F32)


def _sigmoid(x):
    return 1.0 / (1.0 + jnp.exp(-x))


def _silu(x):
    return x * _sigmoid(x)


def _adaln_kernel(c_ref, w_ref, b_ref, o_ref):
    o_ref[...] = _dot(_silu(c_ref[...]), w_ref[...]) + b_ref[...]


def _adaln(c_all, ada_w, ada_b):
    depth, d, n = ada_w.shape
    tn = _row_tile(n, 1024)
    rows = c_all.shape[0]
    return pl.pallas_call(
        _adaln_kernel,
        grid=(depth, n // tn),
        in_specs=[pl.BlockSpec((rows, d), lambda l, j: (0, 0)),
                  pl.BlockSpec((None, d, tn), lambda l, j: (l, 0, j)),
                  pl.BlockSpec((None, 1, tn), lambda l, j: (l, 0, j))],
        out_specs=pl.BlockSpec((None, rows, tn), lambda l, j: (l, 0, j)),
        out_shape=jax.ShapeDtypeStruct((depth, rows, n), F32),
        compiler_params=_params("arbitrary", "arbitrary"),
        name="adaln",
    )(c_all, ada_w, ada_b.reshape(depth, 1, n))


def _rms(x):
    return x * lax.rsqrt(jnp.mean(x * x, axis=-1, keepdims=True) + NORM_EPS)


def _modulate_kernel(x_ref, g_ref, shift_ref, scale_ref, o_ref):
    y = _rms(x_ref[...]) * g_ref[...]
    o_ref[...] = (y * (1.0 + scale_ref[...]) + shift_ref[...]).astype(o_ref.dtype)


def _mod_spec(mod, kind, chunk, width, tm, rows_per_batch):
    d6 = mod.shape[-1]
    per_chunk = (d6 // 6) // width
    if kind == "batch":
        return pl.BlockSpec((None, 1, width),
                            lambda i, j: ((i * tm) // rows_per_batch, 0, chunk * per_chunk + j))
    return pl.BlockSpec((tm, width), lambda i, j: (i, chunk * per_chunk + j))


def _modulate(x, g, mod, kind, shift_chunk, scale_chunk, tm, rows_per_batch):
    m, d = x.shape
    return pl.pallas_call(
        _modulate_kernel,
        grid=(m // tm, 1),
        in_specs=[pl.BlockSpec((tm, d), lambda i, j: (i, 0)),
                  pl.BlockSpec((1, d), lambda i, j: (0, 0)),
                  _mod_spec(mod, kind, shift_chunk, d, tm, rows_per_batch),
                  _mod_spec(mod, kind, scale_chunk, d, tm, rows_per_batch)],
        out_specs=pl.BlockSpec((tm, d), lambda i, j: (i, 0)),
        out_shape=jax.ShapeDtypeStruct((m, d), BF16),
        compiler_params=_params("arbitrary", "arbitrary"),
        name="modulate",
    )(x, g.reshape(1, d), mod, mod)


def _mm_kernel(x_ref, w_ref, o_ref, *, act):
    acc = _dot(x_ref[...], w_ref[...])
    if act == "sigmoid":
        acc = _sigmoid(acc)
    o_ref[...] = acc.astype(o_ref.dtype)


def _w_spec(w, tn, col_of):
    lead = w.ndim - 2
    assert all(s == 1 for s in w.shape[:lead])
    return pl.BlockSpec((None,) * lead + (w.shape[-2], tn), lambda *idx: (0,) * lead + (0, col_of(*idx)))


def _mm(x, w, *, col0=0, n_out=None, tm, tn, act=None, out_dtype=F32, name="mm"):
    m, k = x.shape
    n = w.shape[-1] if n_out is None else n_out
    return pl.pallas_call(
        functools.partial(_mm_kernel, act=act),
        grid=(n // tn, m // tm),
        in_specs=[pl.BlockSpec((tm, k), lambda j, i: (i, 0)),
                  _w_spec(w, tn, lambda j, i: col0 // tn + j)],
        out_specs=pl.BlockSpec((tm, tn), lambda j, i: (i, j)),
        out_shape=jax.ShapeDtypeStruct((m, n), out_dtype),
        compiler_params=_params("arbitrary", "arbitrary"),
        name=name,
    )(x, w)


def _mm_rope_kernel(x_ref, w_ref, cos_ref, sin_ref, o_ref, *, tn, rope_tiles):
    acc = _dot(x_ref[...], w_ref[...])

    @pl.when(pl.program_id(0) < rope_tiles)
    def _():
        cos = cos_ref[...]
        sin = sin_ref[...]
        for s in range(tn // NSA_HEAD_DIM):
            sl = slice(s * NSA_HEAD_DIM, (s + 1) * NSA_HEAD_DIM)
            a = acc[:, sl]
            o_ref[:, sl] = (a * cos + pltpu.roll(a, NSA_HEAD_DIM // 2, 1) * sin).astype(o_ref.dtype)

    @pl.when(pl.program_id(0) >= rope_tiles)
    def _():
        o_ref[...] = acc.astype(o_ref.dtype)


def _mm_rope(x, w, cos, sin, *, col0, n_out, rope_cols, tm, tn, out_dtype, name):
    m, k = x.shape
    pos_tiles = cos.shape[0] // tm
    return pl.pallas_call(
        functools.partial(_mm_rope_kernel, tn=tn, rope_tiles=rope_cols // tn),
        grid=(n_out // tn, m // tm),
        in_specs=[pl.BlockSpec((tm, k), lambda j, i: (i, 0)),
                  _w_spec(w, tn, lambda j, i: col0 // tn + j),
                  pl.BlockSpec((tm, NSA_HEAD_DIM), lambda j, i: (i % pos_tiles, 0)),
                  pl.BlockSpec((tm, NSA_HEAD_DIM), lambda j, i: (i % pos_tiles, 0))],
        out_specs=pl.BlockSpec((tm, tn), lambda j, i: (i, j)),
        out_shape=jax.ShapeDtypeStruct((m, n_out), out_dtype),
        compiler_params=_params("arbitrary", "arbitrary"),
        name=name,
    )(x, w, cos, sin)


def _mm_res_kernel(a_ref, w_ref, res_ref, gate_ref, o_ref):
    o_ref[...] = res_ref[...] + gate_ref[...] * _dot(a_ref[...], w_ref[...])


def _mm_res(a, w, res, mod, kind, gate_chunk, *, tm, tn, rows_per_batch):
    m, k = a.shape
    n = w.shape[-1]
    tn = _row_tile(n, tn)
    swap = lambda f: (lambda j, i: f(i, j))
    gspec = _mod_spec(mod, kind, gate_chunk, tn, tm, rows_per_batch)
    gspec = pl.BlockSpec(gspec.block_shape, swap(gspec.index_map))
    return pl.pallas_call(
        _mm_res_kernel,
        grid=(n // tn, m // tm),
        in_specs=[pl.BlockSpec((tm, k), lambda j, i: (i, 0)),
                  _w_spec(w, tn, lambda j, i: j),
                  pl.BlockSpec((tm, tn), lambda j, i: (i, j)),
                  gspec],
        out_specs=pl.BlockSpec((tm, tn), lambda j, i: (i, j)),
        out_shape=jax.ShapeDtypeStruct((m, n), F32),
        compiler_params=_params("arbitrary", "arbitrary"),
        name="mm_residual",
    )(a, w, res, mod)


def _swiglu_up_kernel(x_ref, wg_ref, wu_ref, o_ref):
    x = x_ref[...]
    o_ref[...] = (_silu(_dot(x, wg_ref[...])) * _dot(x, wu_ref[...])).astype(o_ref.dtype)


def _swiglu_up(x, wg, wu, *, tm, tn):
    m, k = x.shape
    f = wg.shape[-1]
    return pl.pallas_call(
        _swiglu_up_kernel,
        grid=(pl.cdiv(f, tn), m // tm),
        in_specs=[pl.BlockSpec((tm, k), lambda j, i: (i, 0)),
                  _w_spec(wg, tn, lambda j, i: j),
                  _w_spec(wu, tn, lambda j, i: j)],
        out_specs=pl.BlockSpec((tm, tn), lambda j, i: (i, j)),
        out_shape=jax.ShapeDtypeStruct((m, f), BF16),
        compiler_params=_params("arbitrary", "arbitrary"),
        name="swiglu_up",
    )(x, wg, wu)


def _grouped_kernel(meta_ref, *refs, body, n_blk):
    @pl.when(pl.program_id(1) < meta_ref[n_blk])
    def _():
        body(*refs)


def _grouped_specs(n_blk, tm, k_in, tn, weights):
    row = lambda r, meta: jnp.minimum(r, meta[n_blk] - 1)
    x_spec = pl.BlockSpec((tm, k_in), lambda j, r, meta: (row(r, meta), 0))
    w_specs = [pl.BlockSpec((None, None, w.shape[-2], tn), lambda j, r, meta: (0, meta[row(r, meta)], 0, j))
               for w in weights]
    o_spec = pl.BlockSpec((tm, tn), lambda j, r, meta: (row(r, meta), j))
    return [x_spec] + w_specs, o_spec


def _moe_up(xb, meta, wg, wu, *, tm, tn):
    rows, k = xb.shape
    f = wg.shape[-1]
    tn = _row_tile(f, tn)
    n_blk = rows // tm
    in_specs, o_spec = _grouped_specs(n_blk, tm, k, tn, [wg, wu])
    grid_spec = pltpu.PrefetchScalarGridSpec(
        num_scalar_prefetch=1, grid=(f // tn, n_blk), in_specs=in_specs, out_specs=o_spec)
    return pl.pallas_call(
        functools.partial(_grouped_kernel, body=_swiglu_up_kernel, n_blk=n_blk),
        grid_spec=grid_spec,
        out_shape=jax.ShapeDtypeStruct((rows, f), BF16),
        compiler_params=_params("arbitrary", "arbitrary"),
        name="moe_up",
    )(meta, xb, wg, wu)


def _mm_plain_kernel(a_ref, w_ref, o_ref):
    o_ref[...] = _dot(a_ref[...], w_ref[...])


def _moe_down(act, meta, wd, *, tm, tn):
    rows, f = act.shape
    d = wd.shape[-1]
    tn = _row_tile(d, tn)
    n_blk = rows // tm
    in_specs, o_spec = _grouped_specs(n_blk, tm, f, tn, [wd])
    grid_spec = pltpu.PrefetchScalarGridSpec(
        num_scalar_prefetch=1, grid=(d // tn, n_blk), in_specs=in_specs, out_specs=o_spec)
    return pl.pallas_call(
        functools.partial(_grouped_kernel, body=_mm_plain_kernel, n_blk=n_blk),
        grid_spec=grid_spec,
        out_shape=jax.ShapeDtypeStruct((rows, d), F32),
        compiler_params=_params("arbitrary", "arbitrary"),
        name="moe_down",
    )(meta, act, wd)


def _router_kernel(x_ref, w_ref, o_ref):
    lg = _dot(x_ref[...], w_ref[...])
    lane = lax.broadcasted_iota(jnp.int32, lg.shape, 1)
    x = jnp.where(lane < N_EXPERTS, lg, -jnp.inf)
    v0 = jnp.max(x, axis=-1, keepdims=True)
    i0 = jnp.min(jnp.where(x == v0, lane, LANES), axis=-1, keepdims=True)
    x1 = jnp.where(lane == i0, -jnp.inf, x)
    v1 = jnp.max(x1, axis=-1, keepdims=True)
    i1 = jnp.min(jnp.where(x1 == v1, lane, LANES), axis=-1, keepdims=True)
    e1 = jnp.exp(v1 - v0)
    g0 = 1.0 / (1.0 + e1)
    g1 = e1 / (1.0 + e1)
    o_ref[...] = jnp.where(lane == 0, g0,
                           jnp.where(lane == 1, g1,
                                     jnp.where(lane == 2, i0.astype(F32),
                                               jnp.where(lane == 3, i1.astype(F32), 0.0))))


def _router(h, w_router, tm):
    n, d = h.shape
    w_pad = jnp.pad(w_router, ((0, 0), (0, LANES - N_EXPERTS)))
    out = pl.pallas_call(
        _router_kernel,
        grid=(n // tm,),
        in_specs=[pl.BlockSpec((tm, d), lambda i: (i, 0)),
                  pl.BlockSpec((d, LANES), lambda i: (0, 0))],
        out_specs=pl.BlockSpec((tm, LANES), lambda i: (i, 0)),
        out_shape=jax.ShapeDtypeStruct((n, LANES), F32),
        compiler_params=_params("arbitrary"),
        name="moe_router",
    )(h, w_pad)
    return out[:, :TOP_K], out[:, TOP_K:2 * TOP_K].astype(jnp.int32)


def _combine_norm_kernel(x_ref, y0_ref, y1_ref, w_ref, gate_ref, g_ref, o_ref):
    w = w_ref[...]
    y = w[:, 0:1] * y0_ref[...] + w[:, 1:2] * y1_ref[...]
    xo = x_ref[...] + gate_ref[...] * y
    o_ref[...] = _rms(xo) * g_ref[...]


def _combine_norm(x, y0, y1, wts, mod, kind, gate_chunk, g, *, tm, rows_per_batch):
    m, d = x.shape
    row = pl.BlockSpec((tm, d), lambda i, j: (i, 0))
    return pl.pallas_call(
        _combine_norm_kernel,
        grid=(m // tm, 1),
        in_specs=[row, row, row,
                  pl.BlockSpec((tm, TOP_K), lambda i, j: (i, 0)),
                  _mod_spec(mod, kind, gate_chunk, d, tm, rows_per_batch),
                  pl.BlockSpec((1, d), lambda i, j: (0, 0))],
        out_specs=row,
        out_shape=jax.ShapeDtypeStruct((m, d), F32),
        compiler_params=_params("arbitrary", "arbitrary"),
        name="moe_combine_final_norm",
    )(x, y0, y1, wts, mod, g.reshape(1, d))


def _compress_kernel(*refs, n_in):
    w = refs[n_in][...]
    o_ref = refs[n_in + 1]
    outs = []
    for x_ref in refs[:n_in]:
        x = x_ref[...]
        nb = x.shape[0] // NSA_BLOCK
        outs.append(jnp.sum(x.reshape(nb, NSA_BLOCK, x.shape[1]) * w[None], axis=1))
    o_ref[...] = outs[0] if n_in == 1 else jnp.concatenate(outs, axis=0)


def _pool_weights(w_pool):
    w = jnp.transpose(w_pool, (1, 0, 2))[:, :, None, :]
    w = jnp.broadcast_to(w, (NSA_BLOCK, 2, NSA_KV_GROUPS, NSA_HEAD_DIM))
    return w.reshape(NSA_BLOCK, 2 * NSA_KV_WIDTH)


def _compress_prompt(kv, w_full, rows_per_step):
    b, t, kvw = kv.shape
    nb = rows_per_step // NSA_BLOCK
    return pl.pallas_call(
        functools.partial(_compress_kernel, n_in=1),
        grid=(b, t // rows_per_step),
        in_specs=[pl.BlockSpec((None, rows_per_step, kvw), lambda i, s: (i, s, 0)),
                  pl.BlockSpec((NSA_BLOCK, kvw), lambda i, s: (0, 0))],
        out_specs=pl.BlockSpec((None, nb, kvw), lambda i, s: (i, s, 0)),
        out_shape=jax.ShapeDtypeStruct((b, t // NSA_BLOCK, kvw), F32),
        compiler_params=_params("arbitrary", "arbitrary"),
        name="nsa_compress_prompt",
    )(kv, w_full)


def _compress_pages(cache, page_table, w_full, pages_per_step):
    bsz, n_pages = page_table.shape
    kvw = cache.shape[-1]
    p = pages_per_step
    bpp = PAGE_SIZE // NSA_BLOCK

    def page_spec(k):
        return pl.BlockSpec((None, PAGE_SIZE, kvw), lambda i, s, pt: (pt[i, s * p + k], 0, 0))

    grid_spec = pltpu.PrefetchScalarGridSpec(
        num_scalar_prefetch=1,
        grid=(bsz, n_pages // p),
        in_specs=[page_spec(k) for k in range(p)] + [pl.BlockSpec((NSA_BLOCK, kvw), lambda i, s, pt: (0, 0))],
        out_specs=pl.BlockSpec((None, p * bpp, kvw), lambda i, s, pt: (i, s, 0)))

    def body(pt_ref, *refs):
        del pt_ref
        _compress_kernel(*refs, n_in=p)

    return pl.pallas_call(
        body,
        grid_spec=grid_spec,
        out_shape=jax.ShapeDtypeStruct((bsz, n_pages * bpp, kvw), F32),
        compiler_params=_params("arbitrary", "arbitrary"),
        name="nsa_compress_pages",
    )(page_table, *([cache] * p), w_full)


def _select_blocks(score, n_valid):
    rows, width = score.shape
    col = lax.broadcasted_iota(jnp.int32, (rows, width), 1)
    rank = jnp.zeros((rows, width), F32)
    for m in range(n_valid):
        c = score[:, m:m + 1]
        beats = jnp.where(c > score, 1.0, jnp.where((c == score) & (col > m), 1.0, 0.0))
        rank = rank + beats
    return jnp.where((rank < NSA_TOP_BLOCKS) & (col < n_valid), 1.0, 0.0)


def _softmax_rows(s):
    e = jnp.exp(s - jnp.max(s, axis=-1, keepdims=True))
    return e, jnp.sum(e, axis=-1, keepdims=True)


def _compressed_branch(q4, ck, cv, tpos, reps):
    scale = NSA_HEAD_DIM ** -0.5
    nc = ck.shape[0]
    tq = q4.shape[0] // reps
    s = _dot_nt(q4, ck) * scale
    blk = lax.broadcasted_iota(jnp.int32, (1, nc), 1)
    ok = ((blk + 1) * NSA_BLOCK - 1) <= tpos
    e, l = _softmax_rows(jnp.where(ok, s, NEG_INF))
    p = jnp.where(ok, e / l, 0.0)
    o_c = _dot(p, cv)
    imp = p[0:tq]
    for r in range(1, reps):
        imp = imp + p[r * tq:(r + 1) * tq]
    return o_c, imp


def _compressed_select_t(q4, ck, cv, qpos_row, reps):
    scale = NSA_HEAD_DIM ** -0.5
    nc = ck.shape[0]
    tq = q4.shape[0] // reps
    tpos = jnp.concatenate([qpos_row] * reps, axis=1)
    blk = lax.broadcasted_iota(jnp.int32, (nc, 1), 0)
    s = _dot_nt(ck, q4) * scale
    ok = ((blk + 1) * NSA_BLOCK - 1) <= tpos
    s = jnp.where(ok, s, NEG_INF)
    e = jnp.exp(s - jnp.max(s, axis=0, keepdims=True))
    p = jnp.where(ok, e / jnp.sum(e, axis=0, keepdims=True), 0.0)
    o_c = lax.dot_general(p.astype(BF16), cv.astype(BF16), _TN, preferred_element_type=F32)
    imp = p[:, 0:tq]
    for r in range(1, reps):
        imp = imp + p[:, r * tq:(r + 1) * tq]
    cur = qpos_row // NSA_BLOCK
    forced = (blk == 0) | (blk == cur) | (blk == cur - 1)
    score = jnp.where(forced, FORCE_SCORE, jnp.where(blk <= cur, imp, -1.0))
    row = lax.broadcasted_iota(jnp.int32, (nc, tq), 0)
    rank = jnp.zeros((nc, tq), F32)
    for m in range(nc):
        c = score[m:m + 1, :]
        rank = rank + jnp.where(c > score, 1.0, jnp.where((c == score) & (row > m), 1.0, 0.0))
    return o_c, jnp.where(rank < NSA_TOP_BLOCKS, 1.0, 0.0)


def _block_scores(imp, qpos, n_blocks_pad):
    tq, n_imp = imp.shape
    if n_blocks_pad > n_imp:
        imp = jnp.concatenate([imp, jnp.zeros((tq, n_blocks_pad - n_imp), F32)], axis=1)
    blk = lax.broadcasted_iota(jnp.int32, (1, n_blocks_pad), 1)
    cur = qpos // NSA_BLOCK
    forced = (blk == 0) | (blk == cur) | (blk == cur - 1)
    return jnp.where(forced, FORCE_SCORE, jnp.where(blk <= cur, imp, -1.0))


def _nsa_prompt_kernel(q_ref, ck_ref, cv_ref, ks_ref, vs_ref, kw_ref, vw_ref, gate_ref, o_ref, *, tq, kc, seq):
    hd, reps = NSA_HEAD_DIM, NSA_REP
    c2 = (hd ** -0.5) * LOG2_E
    q0 = pl.program_id(2) * tq
    qf = q_ref[...]
    q4 = jnp.concatenate([qf[:, r * hd:(r + 1) * hd] for r in range(reps)], axis=0).astype(BF16)
    qpos = q0 + lax.broadcasted_iota(jnp.int32, (tq, 1), 0)
    m_rows = reps * tq
    stack = lambda x: jnp.concatenate([x] * reps, axis=0)

    n_blocks = seq // NSA_BLOCK
    qpos_row = q0 + lax.broadcasted_iota(jnp.int32, (1, tq), 1)
    o_c, sel = _compressed_select_t(q4, ck_ref[...], cv_ref[...], qpos_row, reps)
    sel = sel.astype(BF16)

    blk_row = lax.broadcasted_iota(jnp.int32, (n_blocks, 1), 0)

    def chunk(c, carry):
        m_i, l_i, acc = carry
        k0 = pl.multiple_of(c * kc, kc)
        kk = ks_ref[pl.ds(k0, kc), :]
        vv = vs_ref[pl.ds(k0, kc), :]
        kpos = k0 + lax.broadcasted_iota(jnp.int32, (1, kc), 1)
        expand = jnp.where(kpos // NSA_BLOCK == blk_row, 1.0, 0.0).astype(BF16)
        selk = lax.dot_general(sel, expand, _TN, preferred_element_type=F32)
        bias = jnp.where(kpos <= qpos, (1.0 - selk) * NEG_INF, NEG_INF)
        s = _dot_nt(q4, kk) * c2 + stack(bias)
        m_new = jnp.maximum(m_i, jnp.max(s, axis=-1, keepdims=True))
        alpha = jnp.exp2(m_i - m_new)
        p = jnp.exp2(s - m_new)
        l_new = alpha * l_i + jnp.sum(p, axis=-1, keepdims=True)
        return m_new, l_new, alpha * acc + _dot(p, vv)

    n_chunks = (q0 + tq + kc - 1) // kc
    init = (jnp.full((m_rows, 1), NEG_INF, F32), jnp.zeros((m_rows, 1), F32), jnp.zeros((m_rows, hd), F32))
    _, l_s, acc_s = lax.fori_loop(0, n_chunks, chunk, init)
    o_s = acc_s / l_s

    span = NSA_WINDOW + tq
    w0 = pl.multiple_of(jnp.maximum(q0 - NSA_WINDOW, 0), tq)
    dist = qpos - (w0 + lax.broadcasted_iota(jnp.int32, (1, span), 1))
    bias = jnp.where(dist >= 0, jnp.where(dist < NSA_WINDOW, 0.0, NEG_INF), NEG_INF)
    s = _dot_nt(q4, kw_ref[pl.ds(w0, span), :]) * c2 + stack(bias)
    e = jnp.exp2(s - jnp.max(s, axis=-1, keepdims=True))
    o_w = _dot(e, vw_ref[pl.ds(w0, span), :]) / jnp.sum(e, axis=-1, keepdims=True)

    gate = gate_ref[...]
    for r in range(reps):
        rows = slice(r * tq, (r + 1) * tq)
        o = (gate[:, 3 * r:3 * r + 1] * o_c[rows] + gate[:, 3 * r + 1:3 * r + 2] * o_s[rows]
             + gate[:, 3 * r + 2:3 * r + 3] * o_w[rows])
        o_ref[:, r * hd:(r + 1) * hd] = o.astype(o_ref.dtype)


def _nsa_prompt_attention(q, ckv, kv_s, kv_w, gates, *, tq, kc):
    b, t, _ = q.shape
    g_, hd = NSA_KV_GROUPS, NSA_HEAD_DIM
    qw = NSA_REP * hd
    nc = ckv.shape[1]
    keys = pl.BlockSpec((None, t, hd), lambda i, g, s: (i, 0, g))
    vals = pl.BlockSpec((None, t, hd), lambda i, g, s: (i, 0, g_ + g))
    return pl.pallas_call(
        functools.partial(_nsa_prompt_kernel, tq=tq, kc=kc, seq=t),
        grid=(b, g_, t // tq),
        in_specs=[pl.BlockSpec((None, tq, qw), lambda i, g, s: (i, s, g)),
                  pl.BlockSpec((None, nc, hd), lambda i, g, s: (i, 0, g)),
                  pl.BlockSpec((None, nc, hd), lambda i, g, s: (i, 0, g_ + g)),
                  keys, vals, keys, vals,
                  pl.BlockSpec((None, tq, LANES), lambda i, g, s: (i, s, g))],
        out_specs=pl.BlockSpec((None, tq, qw), lambda i, g, s: (i, s, g)),
        out_shape=jax.ShapeDtypeStruct((b, t, NSA_Q_WIDTH), BF16),
        compiler_params=_params("arbitrary", "arbitrary", "arbitrary"),
        name="nsa_prompt_attention",
    )(q, ckv, ckv, kv_s, kv_s, kv_w, kv_w, gates)


def _nsa_sample_select_kernel(q_ref, ckv_ref, oc_ref, sel_ref, *, t_new, past, n_blocks, n_pad):
    hd, reps = NSA_HEAD_DIM, NSA_REP
    qpos = past + lax.broadcasted_iota(jnp.int32, (t_new, 1), 0)
    tpos = jnp.concatenate([qpos] * reps, axis=0)
    for g in range(NSA_KV_GROUPS):
        ck = ckv_ref[:, g * hd:(g + 1) * hd]
        cv = ckv_ref[:, NSA_KV_WIDTH + g * hd:NSA_KV_WIDTH + (g + 1) * hd]
        o_c, imp = _compressed_branch(q_ref[g].astype(BF16), ck, cv, tpos, reps)
        oc_ref[g] = o_c
        sel_ref[g] = _select_blocks(_block_scores(imp, qpos, n_pad), n_blocks)


def _nsa_sample_select(q4, ckv, *, t_new, past, n_blocks, n_pad):
    b = q4.shape[0]
    g_, hd = NSA_KV_GROUPS, NSA_HEAD_DIM
    rows = NSA_REP * t_new
    nc = ckv.shape[1]
    return pl.pallas_call(
        functools.partial(_nsa_sample_select_kernel, t_new=t_new, past=past, n_blocks=n_blocks, n_pad=n_pad),
        grid=(b,),
        in_specs=[pl.BlockSpec((None, g_, rows, hd), lambda i: (i, 0, 0, 0)),
                  pl.BlockSpec((None, nc, 2 * NSA_KV_WIDTH), lambda i: (i, 0, 0))],
        out_specs=[pl.BlockSpec((None, g_, rows, hd), lambda i: (i, 0, 0, 0)),
                   pl.BlockSpec((None, g_, t_new, n_pad), lambda i: (i, 0, 0, 0))],
        out_shape=[jax.ShapeDtypeStruct((b, g_, rows, hd), F32),
                   jax.ShapeDtypeStruct((b, g_, t_new, n_pad), F32)],
        compiler_params=_params("arbitrary"),
        name="nsa_sample_select",
    )(q4, ckv)


def _nsa_sample_attend_kernel(pt_ref, *refs, n_pg, t_new, past, n_pad):
    del pt_ref
    pages = refs[:n_pg]
    (q_ref, sel_ref, oc_ref, new_s_ref, win_ref, new_w_ref, gate_ref,
     o_ref, win_out_ref, m_ref, l_ref, acc_ref) = refs[n_pg:]
    hd, reps, g_ = NSA_HEAD_DIM, NSA_REP, NSA_KV_GROUPS
    kvw = NSA_KV_WIDTH
    scale = hd ** -0.5
    step = pl.program_id(1)
    rows = reps * t_new
    qpos = past + lax.broadcasted_iota(jnp.int32, (t_new, 1), 0)
    tpos = jnp.concatenate([qpos] * reps, axis=0)
    nbp = past // NSA_BLOCK

    def online(g, s, vv):
        m_i = m_ref[g]
        m_new = jnp.maximum(m_i, jnp.max(s, axis=-1, keepdims=True))
        alpha = jnp.exp(m_i - m_new)
        p = jnp.exp(s - m_new)
        l_ref[g] = alpha * l_ref[g] + jnp.sum(p, axis=-1, keepdims=True)
        acc_ref[g] = alpha * acc_ref[g] + _dot(p, vv)
        m_ref[g] = m_new

    def masked(sel_g, first_blk, width, kpos, s):
        blk_row = lax.broadcasted_iota(jnp.int32, (n_pad, 1), 0)
        key_blk = first_blk + lax.broadcasted_iota(jnp.int32, (1, width), 1) // NSA_BLOCK
        expand = jnp.where(key_blk == blk_row, 1.0, 0.0).astype(BF16)
        selk = jnp.dot(sel_g.astype(BF16), expand, preferred_element_type=F32)
        selk = jnp.concatenate([selk] * reps, axis=0)
        return jnp.where(kpos <= tpos, jnp.where(selk > 0.5, s, NEG_INF), NEG_INF)

    def new_rows(ref, c0):
        x = ref[:, c0:c0 + hd]
        return jnp.concatenate([x, jnp.zeros((PAGE_SIZE - t_new, hd), x.dtype)], axis=0)

    @pl.when(step == 0)
    def _():
        m_ref[...] = jnp.full(m_ref.shape, NEG_INF, F32)
        l_ref[...] = jnp.zeros(l_ref.shape, F32)
        acc_ref[...] = jnp.zeros(acc_ref.shape, F32)
        kpos = past + lax.broadcasted_iota(jnp.int32, (1, PAGE_SIZE), 1)
        for g in range(g_):
            kk = new_rows(new_s_ref, g * hd)
            vv = new_rows(new_s_ref, kvw + g * hd)
            s = _dot_nt(q_ref[g], kk) * scale
            online(g, masked(sel_ref[g], nbp, PAGE_SIZE, kpos, s), vv)

    width = n_pg * PAGE_SIZE
    first_key = step * width
    kpos = first_key + lax.broadcasted_iota(jnp.int32, (1, width), 1)
    for g in range(g_):
        kk = jnp.concatenate([pg[:, g * hd:(g + 1) * hd].astype(BF16) for pg in pages], axis=0)
        vv = jnp.concatenate([pg[:, kvw + g * hd:kvw + (g + 1) * hd].astype(BF16) for pg in pages], axis=0)
        s = _dot_nt(q_ref[g], kk) * scale
        online(g, masked(sel_ref[g], first_key // NSA_BLOCK, width, kpos, s), vv)

    @pl.when(step == pl.num_programs(1) - 1)
    def _():
        wb = win_ref.shape[0]
        wpos = past - wb + lax.broadcasted_iota(jnp.int32, (1, wb + PAGE_SIZE), 1)
        dist = tpos - wpos
        gate = gate_ref[...]
        for g in range(g_):
            kk = jnp.concatenate([win_ref[:, g * hd:(g + 1) * hd], new_rows(new_w_ref, g * hd)], axis=0)
            vv = jnp.concatenate([win_ref[:, kvw + g * hd:kvw + (g + 1) * hd],
                                  new_rows(new_w_ref, kvw + g * hd)], axis=0)
            s = _dot_nt(q_ref[g], kk) * scale
            ok = (wpos >= 0) & (dist >= 0) & (dist < NSA_WINDOW)
            e, l = _softmax_rows(jnp.where(ok, s, NEG_INF))
            o_w = _dot(e, vv) / l
            o_s = acc_ref[g] / l_ref[g]
            o_c = oc_ref[g]
            for r in range(reps):
                rs = slice(r * t_new, (r + 1) * t_new)
                c0 = g * LANES + 3 * r
                o = (gate[:, c0:c0 + 1] * o_c[rs] + gate[:, c0 + 1:c0 + 2] * o_s[rs]
                     + gate[:, c0 + 2:c0 + 3] * o_w[rs])
                col = (g * reps + r) * hd
                o_ref[:, col:col + hd] = o.astype(o_ref.dtype)
        win_out_ref[0:wb - t_new, :] = win_ref[t_new:wb, :]
        win_out_ref[wb - t_new:wb, :] = new_w_ref[...]


def _nsa_sample_attend(q4, sel, o_c, kv_s, win, kv_w, gates, cache, page_table, *, n_pg, past):
    b, g_, rows, hd = q4.shape
    t_new = rows // NSA_REP
    n_pad = sel.shape[-1]
    kvw2 = 2 * NSA_KV_WIDTH
    n_pages = page_table.shape[1]
    wb = win.shape[1]

    def page_spec(k):
        return pl.BlockSpec((None, PAGE_SIZE, kvw2), lambda i, s, pt: (pt[i, s * n_pg + k], 0, 0))

    per_b = lambda shape: pl.BlockSpec((None,) + shape, lambda i, s, pt: (i,) + (0,) * len(shape))
    grid_spec = pltpu.PrefetchScalarGridSpec(
        num_scalar_prefetch=1,
        grid=(b, n_pages // n_pg),
        in_specs=[page_spec(k) for k in range(n_pg)] + [
            per_b((g_, rows, hd)), per_b((g_, t_new, n_pad)), per_b((g_, rows, hd)),
            per_b((t_new, kvw2)), per_b((wb, kvw2)), per_b((t_new, kvw2)),
            per_b((t_new, g_ * LANES))],
        out_specs=[per_b((t_new, NSA_Q_WIDTH)), per_b((wb, kvw2))],
        scratch_shapes=[pltpu.VMEM((g_, rows, 1), F32), pltpu.VMEM((g_, rows, 1), F32),
                        pltpu.VMEM((g_, rows, hd), F32)])
    return pl.pallas_call(
        functools.partial(_nsa_sample_attend_kernel, n_pg=n_pg, t_new=t_new, past=past, n_pad=n_pad),
        grid_spec=grid_spec,
        out_shape=[jax.ShapeDtypeStruct((b, t_new, NSA_Q_WIDTH), BF16),
                   jax.ShapeDtypeStruct((b, wb, kvw2), F32)],
        compiler_params=_params("arbitrary", "arbitrary"),
        name="nsa_sample_attention",
    )(page_table, *([cache] * n_pg), q4, sel, o_c, kv_s, win, kv_w, gates)


def _log_sigmoid(x):
    return jnp.minimum(x, 0.0) - jnp.log1p(jnp.exp(-jnp.abs(x)))


def _mlstm_kernel(q_ref, k_ref, v_ref, o_ref, gc_ref, gr_ref, bc_ref, br_ref, ng_ref,
                  c0_ref, n0_ref, m0_ref, y_ref, c_out, n_out, m_out, c_s, n_s, m_s):
    nh, dk, dv = MLSTM_HEADS, MLSTM_QK_DIM, MLSTM_V_DIM
    step = pl.program_id(1)
    ln = q_ref.shape[0]

    @pl.when(step == 0)
    def _():
        c_s[...] = c0_ref[...]
        n_s[...] = n0_ref[...]
        m_s[...] = m0_ref[...]

    hi = lax.Precision.HIGHEST
    r_i = lax.broadcasted_iota(jnp.int32, (ln, ln), 0)
    c_i = lax.broadcasted_iota(jnp.int32, (ln, ln), 1)
    causal = c_i <= r_i
    lower = jnp.where(causal, 1.0, 0.0)
    upper = jnp.where(r_i <= c_i, 1.0, 0.0)
    gcol = gc_ref[...] + bc_ref[...]
    grow = gr_ref[...] + br_ref[...]
    ig_c = gcol[:, :nh]
    b_c = jnp.dot(lower, _log_sigmoid(gcol[:, nh:]), precision=hi, preferred_element_type=F32)
    ig_r = grow[:nh, :]
    b_r = jnp.dot(_log_sigmoid(grow[nh:, :]), upper, precision=hi, preferred_element_type=F32)

    for h in range(nh):
        q = q_ref[:, h * dk:(h + 1) * dk]
        k = k_ref[:, h * dk:(h + 1) * dk] * (dk ** -0.5)
        v = v_ref[:, h * dv:(h + 1) * dv]
        qb, kb, vb = q.astype(BF16), k.astype(BF16), v.astype(BF16)
        bt = b_c[:, h:h + 1]
        b_last = b_c[ln - 1:ln, h:h + 1]
        m_prev = m_s[h]
        d = jnp.where(causal, bt - b_r[h:h + 1, :] + ig_r[h:h + 1, :], -jnp.inf)
        inter = bt + m_prev
        m_t = jnp.maximum(inter, jnp.max(d, axis=-1, keepdims=True))
        w = jnp.exp(d - m_t) * _dot_nt(qb, kb)
        a = jnp.exp(inter - m_t)
        c_prev = c_s[h]
        n_prev = n_s[h]
        num = a * _dot(qb, c_prev) + _dot(w, vb)
        den = a * jnp.sum(q * n_prev, axis=-1, keepdims=True) + jnp.sum(w, axis=-1, keepdims=True)
        hh = num / jnp.maximum(jnp.abs(den), jnp.exp(-m_t))
        m_new = m_t[ln - 1:ln, :]
        ws = jnp.exp(b_last - bt + ig_c[:, h:h + 1] - m_new)
        decay = jnp.exp(b_last + m_prev - m_new)
        kw = k * ws
        c_s[h] = decay * c_prev + lax.dot_general(kw.astype(BF16), vb, _TN, preferred_element_type=F32)
        n_s[h] = decay * n_prev + jnp.sum(kw, axis=0, keepdims=True)
        m_s[h] = m_new
        hn = _rms(hh) * ng_ref[:, h * dv:(h + 1) * dv]
        y_ref[:, h * dv:(h + 1) * dv] = (hn * _sigmoid(o_ref[:, h * dv:(h + 1) * dv])).astype(y_ref.dtype)

    @pl.when(step == pl.num_programs(1) - 1)
    def _():
        c_out[...] = c_s[...]
        n_out[...] = n_s[...]
        m_out[...] = m_s[...]


def _mlstm(proj, graw, b_gate, norm_g, c0, n0, m0, *, chunk):
    b, t, _ = proj.shape
    nh, dk, dv = MLSTM_HEADS, MLSTM_QK_DIM, MLSTM_V_DIM
    nchunk = t // chunk
    qw, vw = nh * dk, nh * dv
    grow = jnp.transpose(graw.reshape(b, nchunk, chunk, 2 * nh), (0, 1, 3, 2))
    state = lambda shape: pl.BlockSpec((None,) + shape, lambda i, s: (i,) + (0,) * len(shape))
    const = lambda shape: pl.BlockSpec(shape, lambda i, s: (0,) * len(shape))
    return pl.pallas_call(
        _mlstm_kernel,
        grid=(b, nchunk),
        in_specs=[pl.BlockSpec((None, chunk, qw), lambda i, s: (i, s, 0)),
                  pl.BlockSpec((None, chunk, qw), lambda i, s: (i, s, 1)),
                  pl.BlockSpec((None, chunk, vw), lambda i, s: (i, s, 2 * qw // vw)),
                  pl.BlockSpec((None, chunk, vw), lambda i, s: (i, s, 2 * qw // vw + 1)),
                  pl.BlockSpec((None, chunk, 2 * nh), lambda i, s: (i, s, 0)),
                  pl.BlockSpec((None, None, 2 * nh, chunk), lambda i, s: (i, s, 0, 0)),
                  const((1, 2 * nh)), const((2 * nh, 1)), const((1, vw)),
                  state((nh, dk, dv)), state((nh, 1, dk)), state((nh, 1, 1))],
        out_specs=[pl.BlockSpec((None, chunk, vw), lambda i, s: (i, s, 0)),
                   state((nh, dk, dv)), state((nh, 1, dk)), state((nh, 1, 1))],
        out_shape=[jax.ShapeDtypeStruct((b, t, vw), BF16),
                   jax.ShapeDtypeStruct((b, nh, dk, dv), F32),
                   jax.ShapeDtypeStruct((b, nh, 1, dk), F32),
                   jax.ShapeDtypeStruct((b, nh, 1, 1), F32)],
        scratch_shapes=[pltpu.VMEM((nh, dk, dv), F32), pltpu.VMEM((nh, 1, dk), F32),
                        pltpu.VMEM((nh, 1, 1), F32)],
        compiler_params=_params("arbitrary", "arbitrary"),
        name="mlstm",
    )(proj, proj, proj, proj, graw, grow, b_gate.reshape(1, 2 * nh), b_gate.reshape(2 * nh, 1),
      norm_g.reshape(1, vw), c0, n0.reshape(b, nh, 1, dk), m0.reshape(b, nh, 1, 1))


def _rope_tables(pos):
    half = NSA_HEAD_DIM // 2
    freq = ROPE_THETA ** (-jnp.arange(half, dtype=F32) / half)
    ang = pos.astype(F32)[:, None] * freq[None, :]
    cos, sin = jnp.cos(ang), jnp.sin(ang)
    return jnp.concatenate([cos, cos], axis=-1), jnp.concatenate([-sin, sin], axis=-1)


def _gate_weights(w_in):
    d = w_in.shape[0]
    per_group = NSA_REP * 3
    w = w_in[:, NSA_Q_WIDTH + 6 * NSA_KV_WIDTH:].reshape(d, NSA_KV_GROUPS, per_group)
    w = jnp.pad(w, ((0, 0), (0, 0), (0, LANES - per_group)))
    return w.reshape(d, NSA_KV_GROUPS * LANES)


def _row_tile(m, pref):
    t = min(pref, m)
    while m % t:
        t //= 2
    return t


class _Group:
    def __init__(self, x, mod_rows, kind):
        self.b, self.t, self.d = x.shape
        self.x = x.reshape(self.b * self.t, self.d)
        self.kind = kind
        self.mod_rows = mod_rows

    def mod(self, layer):
        m = self.mod_rows[layer]
        if self.kind == "batch":
            return m.reshape(self.b, 1, m.shape[-1])
        return jnp.repeat(m, self.t, axis=0)


def kernel(x_prompt, x_sample, c_prompt, c_sample, cache_nsa_cmp_kv, cache_nsa_sel_kv, page_table, state_nsa_win_kv, state_mlstm_c, state_mlstm_n, state_mlstm_m, ada_w, ada_b, norm_mix_g, norm_ffn_g, norm_final_g, nsa_w_in, nsa_w_pool, nsa_w_out, mlstm_w_in, mlstm_b_gate, mlstm_norm_g, mlstm_w_out, ffn_w_gate, ffn_w_up, ffn_w_down, moe_w_router, moe_w_gate, moe_w_up, moe_w_down):
    bp, tp, d = x_prompt.shape
    bs, ts, _ = x_sample.shape
    past = page_table.shape[1] * PAGE_SIZE
    assert tp % NSA_BLOCK == 0 and tp >= NSA_WINDOW + 128 and ts < NSA_BLOCK and ts % SUBLANES == 0
    assert past % NSA_BLOCK == 0 and state_nsa_win_kv.shape[2] == NSA_WINDOW
    assert ada_w.shape[0] == 2 and nsa_w_in.shape[0] == 1 and mlstm_w_in.shape[0] == 1

    mod_all = _adaln(jnp.concatenate([c_prompt, c_sample], axis=0), ada_w, ada_b)
    groups = [_Group(x_prompt, mod_all[:, :bp], "batch"), _Group(x_sample, mod_all[:, bp:], "row")]
    tms = [_row_tile(tp, 1024), bs * ts]
    tes = [_row_tile(tp, 512), bs * ts]

    w_gates = _gate_weights(nsa_w_in[0])
    w_full = _pool_weights(nsa_w_pool[0])
    g_, hd = NSA_KV_GROUPS, NSA_HEAD_DIM
    kvw2 = 2 * NSA_KV_WIDTH
    outs = {}
    for gi, (grp, tm, te) in enumerate(zip(groups, tms, tes)):
        b, t = grp.b, grp.t
        mod = grp.mod(0)
        h = _modulate(grp.x, norm_mix_g[0], mod, grp.kind, 0, 1, te, t)
        pos = jnp.arange(t) if gi == 0 else past + jnp.arange(t)
        cos, sin = _rope_tables(pos)
        if gi == 1:
            cos, sin = jnp.tile(cos, (b, 1)), jnp.tile(sin, (b, 1))
        q = _mm_rope(h, nsa_w_in, cos, sin, col0=0, n_out=NSA_Q_WIDTH, rope_cols=NSA_Q_WIDTH,
                     tm=tm, tn=512, out_dtype=BF16, name="nsa_q_proj").reshape(b, t, NSA_Q_WIDTH)
        kv_c, kv_s, kv_w = [
            _mm_rope(h, nsa_w_in, cos, sin, col0=NSA_Q_WIDTH + br * kvw2, n_out=kvw2, rope_cols=NSA_KV_WIDTH,
                     tm=tm, tn=512, out_dtype=F32, name="nsa_kv_proj").reshape(b, t, kvw2)
            for br in range(3)]
        gates = _mm(h, w_gates, tm=tm, tn=g_ * LANES, act="sigmoid", name="nsa_gates").reshape(b, t, g_ * LANES)
        if gi == 0:
            ckv = _compress_prompt(kv_c, w_full, 8 * NSA_BLOCK)
            att = _nsa_prompt_attention(q, ckv, kv_s, kv_w, gates, tq=128, kc=512)
            win_out = kv_w[:, t - min(NSA_WINDOW, t):]
        else:
            n_phys = cache_nsa_cmp_kv.shape[1]
            ckv = _compress_pages(cache_nsa_cmp_kv.reshape(n_phys, PAGE_SIZE, kvw2), page_table, w_full, 4)
            n_blocks = past // NSA_BLOCK + 1
            n_pad = -(-n_blocks // LANES) * LANES
            q4 = jnp.transpose(q.reshape(b, t, g_, NSA_REP, hd), (0, 2, 3, 1, 4)).reshape(b, g_, NSA_REP * t, hd)
            o_c, sel = _nsa_sample_select(q4, ckv, t_new=t, past=past, n_blocks=n_blocks, n_pad=n_pad)
            att, win_out = _nsa_sample_attend(
                q4, sel, o_c, kv_s, state_nsa_win_kv.reshape(b, NSA_WINDOW, kvw2), kv_w, gates,
                cache_nsa_sel_kv.reshape(n_phys, PAGE_SIZE, kvw2), page_table, n_pg=8, past=past)
        as_cache = lambda a: a.reshape(1, b, a.shape[1], 2, g_, hd)
        outs[gi] = (as_cache(kv_c), as_cache(kv_s), as_cache(win_out))
        x1 = _mm_res(att.reshape(b * t, NSA_Q_WIDTH), nsa_w_out, grp.x, mod, grp.kind, 2,
                     tm=tm, tn=512, rows_per_batch=t)
        h2 = _modulate(x1, norm_ffn_g[0], mod, grp.kind, 3, 4, te, t)
        act = _swiglu_up(h2, ffn_w_gate, ffn_w_up, tm=tm, tn=512)
        grp.x = _mm_res(act, ffn_w_down, x1, mod, grp.kind, 5, tm=te, tn=512, rows_per_batch=t)

    nh, dk, dv = MLSTM_HEADS, MLSTM_QK_DIM, MLSTM_V_DIM
    n_main = 2 * nh * dk + 2 * nh * dv
    w_g = jnp.pad(mlstm_w_in[0][:, n_main:], ((0, 0), (0, LANES - 2 * nh)))
    states = {}
    hs2, mods = [], []
    for gi, (grp, tm, te) in enumerate(zip(groups, tms, tes)):
        b, t = grp.b, grp.t
        mod = grp.mod(1)
        h = _modulate(grp.x, norm_mix_g[1], mod, grp.kind, 0, 1, te, t)
        proj = _mm(h, mlstm_w_in, n_out=n_main, tm=tm, tn=512, name="mlstm_in_proj").reshape(b, t, n_main)
        graw = _mm(h, w_g, tm=tm, tn=LANES, name="mlstm_gates")[:, :2 * nh].reshape(b, t, 2 * nh)
        if gi == 0:
            c0 = jnp.zeros((b, nh, dk, dv), F32)
            n0 = jnp.zeros((b, nh, dk), F32)
            m0 = jnp.zeros((b, nh), F32)
            chunk = MLSTM_CHUNK
        else:
            c0, n0, m0 = state_mlstm_c[0], state_mlstm_n[0], state_mlstm_m[0]
            chunk = t
        y, c_new, n_new, m_new = _mlstm(proj, graw, mlstm_b_gate[0], mlstm_norm_g[0], c0, n0, m0, chunk=chunk)
        states[gi] = (c_new[None], n_new.reshape(1, b, nh, dk), m_new.reshape(1, b, nh))
        grp.x = _mm_res(y.reshape(b * t, nh * dv), mlstm_w_out, grp.x, mod, grp.kind, 2,
                        tm=tm, tn=512, rows_per_batch=t)
        hs2.append(_modulate(grp.x, norm_ffn_g[1], mod, grp.kind, 3, 4, te, t))
        mods.append(mod)

    h_all = jnp.concatenate(hs2, axis=0)
    n_tok = h_all.shape[0]
    tm_moe = MOE_ROW_BLOCK
    wts, top_e = _router(h_all, moe_w_router[0], _row_tile(n_tok, 256))
    n_asg = n_tok * TOP_K
    flat_e = top_e.reshape(n_asg)
    onehot = (flat_e[:, None] == jnp.arange(N_EXPERTS)[None, :]).astype(jnp.int32)
    within = jnp.take_along_axis(jnp.cumsum(onehot, axis=0), flat_e[:, None], axis=1)[:, 0] - 1
    counts = jnp.sum(onehot, axis=0)
    padded = (counts + tm_moe - 1) // tm_moe * tm_moe
    p_end = jnp.cumsum(padded)
    dest = (p_end - padded)[flat_e] + within
    n_blk = -(-(n_asg + N_EXPERTS * (tm_moe - 1)) // tm_moe)
    n_rows = n_blk * tm_moe
    row_tok = jnp.full((n_rows,), n_tok, jnp.int32).at[dest].set(jnp.arange(n_asg, dtype=jnp.int32) // TOP_K)
    blk_start = jnp.arange(n_blk, dtype=jnp.int32) * tm_moe
    blk_e = jnp.minimum(jnp.sum((p_end[None, :] <= blk_start[:, None]).astype(jnp.int32), axis=1), N_EXPERTS - 1)
    meta = jnp.concatenate([blk_e, p_end[-1:] // tm_moe]).astype(jnp.int32)
    xb = jnp.concatenate([h_all, jnp.zeros((1, d), h_all.dtype)], axis=0)[row_tok]
    act = _moe_up(xb, meta, moe_w_gate, moe_w_up, tm=tm_moe, tn=1024)
    yb = _moe_down(act, meta, moe_w_down, tm=tm_moe, tn=512)
    dest2 = dest.reshape(n_tok, TOP_K)

    finals = []
    off = 0
    for gi, (grp, te) in enumerate(zip(groups, tes)):
        m = grp.b * grp.t
        dg = dest2[off:off + m]
        fin = _combine_norm(grp.x, yb[dg[:, 0]], yb[dg[:, 1]], wts[off:off + m], mods[gi], grp.kind, 5,
                            norm_final_g, tm=te, rows_per_batch=grp.t)
        finals.append(fin.reshape(grp.b, grp.t, d))
        off += m

    return (finals[0], finals[1],
            outs[0][0], outs[0][1], outs[0][2], states[0][0], states[0][1], states[0][2],
            outs[1][0], outs[1][1], outs[1][2], states[1][0], states[1][1], states[1][2])
```

```python
import functools

import jax
import jax.numpy as jnp
from jax import lax
from jax.experimental import pallas as pl
from jax.experimental.pallas import tpu as pltpu

F32 = jnp.float32
BF16 = jnp.bfloat16

PAGE_SIZE = 128
NSA_HEADS = 16
NSA_HEAD_DIM = 128
NSA_KV_GROUPS = 4
NSA_REP = NSA_HEADS // NSA_KV_GROUPS
NSA_BLOCK = 64
NSA_TOP_BLOCKS = 16
NSA_WINDOW = 512
NSA_Q_WIDTH = NSA_HEADS * NSA_HEAD_DIM
NSA_KV_WIDTH = NSA_KV_GROUPS * NSA_HEAD_DIM
ROPE_THETA = 10000.0
FORCE_SCORE = 1.0e4
NEG_INF = -1.0e30
MLSTM_HEADS = 8
MLSTM_QK_DIM = 128
MLSTM_V_DIM = 256
MLSTM_CHUNK = 64
N_EXPERTS = 8
TOP_K = 2
MOE_ROW_BLOCK = 512
NORM_EPS = 1e-6
LOG2_E = 1.4426950408889634

LANES = 128
SUBLANES = 8
VMEM_LIMIT_BYTES = 52 * 1024 * 1024

_NT = (((1,), (1,)), ((), ()))
_TN = (((0,), (0,)), ((), ()))


def _params(*sem):
    return pltpu.CompilerParams(dimension_semantics=sem, vmem_limit_bytes=VMEM_LIMIT_BYTES)


def _dot(a, b):
    return jnp.dot(a.astype(BF16), b.astype(BF16), preferred_element_type=F32)


def _dot_nt(a, b):
    return lax.dot_general(a.astype(BF16), b.astype(BF16), _NT, preferred_element_type=F32)


def _sigmoid(x):
    return 1.0 / (1.0 + jnp.exp(-x))


def _silu(x):
    return x * _sigmoid(x)


def _adaln_kernel(c_ref, w_ref, b_ref, o_ref):
    o_ref[...] = _dot(_silu(c_ref[...]), w_ref[...]) + b_ref[...]


def _adaln(c_all, ada_w, ada_b):
    depth, d, n = ada_w.shape
    tn = _row_tile(n, 1024)
    rows = c_all.shape[0]
    return pl.pallas_call(
        _adaln_kernel,
        grid=(depth, n // tn),
        in_specs=[pl.BlockSpec((rows, d), lambda l, j: (0, 0)),
                  pl.BlockSpec((None, d, tn), lambda l, j: (l, 0, j)),
                  pl.BlockSpec((None, 1, tn), lambda l, j: (l, 0, j))],
        out_specs=pl.BlockSpec((None, rows, tn), lambda l, j: (l, 0, j)),
        out_shape=jax.ShapeDtypeStruct((depth, rows, n), F32),
        compiler_params=_params("arbitrary", "arbitrary"),
        name="adaln",
    )(c_all, ada_w, ada_b.reshape(depth, 1, n))


def _rms(x):
    return x * lax.rsqrt(jnp.mean(x * x, axis=-1, keepdims=True) + NORM_EPS)


def _modulate_kernel(x_ref, g_ref, shift_ref, scale_ref, o_ref):
    y = _rms(x_ref[...]) * g_ref[...]
    o_ref[...] = (y * (1.0 + scale_ref[...]) + shift_ref[...]).astype(o_ref.dtype)


def _mod_spec(mod, kind, chunk, width, tm, rows_per_batch):
    d6 = mod.shape[-1]
    per_chunk = (d6 // 6) // width
    if kind == "batch":
        return pl.BlockSpec((None, 1, width),
                            lambda i, j: ((i * tm) // rows_per_batch, 0, chunk * per_chunk + j))
    return pl.BlockSpec((tm, width), lambda i, j: (i, chunk * per_chunk + j))


def _modulate(x, g, mod, kind, shift_chunk, scale_chunk, tm, rows_per_batch):
    m, d = x.shape
    return pl.pallas_call(
        _modulate_kernel,
        grid=(m // tm, 1),
        in_specs=[pl.BlockSpec((tm, d), lambda i, j: (i, 0)),
                  pl.BlockSpec((1, d), lambda i, j: (0, 0)),
                  _mod_spec(mod, kind, shift_chunk, d, tm, rows_per_batch),
                  _mod_spec(mod, kind, scale_chunk, d, tm, rows_per_batch)],
        out_specs=pl.BlockSpec((tm, d), lambda i, j: (i, 0)),
        out_shape=jax.ShapeDtypeStruct((m, d), BF16),
        compiler_params=_params("arbitrary", "arbitrary"),
        name="modulate",
    )(x, g.reshape(1, d), mod, mod)


def _mm_kernel(x_ref, w_ref, o_ref, *, act):
    acc = _dot(x_ref[...], w_ref[...])
    if act == "sigmoid":
        acc = _sigmoid(acc)
    o_ref[...] = acc.astype(o_ref.dtype)


def _w_spec(w, tn, col_of):
    lead = w.ndim - 2
    assert all(s == 1 for s in w.shape[:lead])
    return pl.BlockSpec((None,) * lead + (w.shape[-2], tn), lambda *idx: (0,) * lead + (0, col_of(*idx)))


def _mm(x, w, *, col0=0, n_out=None, tm, tn, act=None, out_dtype=F32, name="mm"):
    m, k = x.shape
    n = w.shape[-1] if n_out is None else n_out
    return pl.pallas_call(
        functools.partial(_mm_kernel, act=act),
        grid=(n // tn, m // tm),
        in_specs=[pl.BlockSpec((tm, k), lambda j, i: (i, 0)),
                  _w_spec(w, tn, lambda j, i: col0 // tn + j)],
        out_specs=pl.BlockSpec((tm, tn), lambda j, i: (i, j)),
        out_shape=jax.ShapeDtypeStruct((m, n), out_dtype),
        compiler_params=_params("arbitrary", "arbitrary"),
        name=name,
    )(x, w)


def _mm_rope_kernel(x_ref, w_ref, cos_ref, sin_ref, o_ref, *, tn, rope_tiles):
    acc = _dot(x_ref[...], w_ref[...])

    @pl.when(pl.program_id(0) < rope_tiles)
    def _():
        cos = cos_ref[...]
        sin = sin_ref[...]
        for s in range(tn // NSA_HEAD_DIM):
            sl = slice(s * NSA_HEAD_DIM, (s + 1) * NSA_HEAD_DIM)
            a = acc[:, sl]
            o_ref[:, sl] = (a * cos + pltpu.roll(a, NSA_HEAD_DIM // 2, 1) * sin).astype(o_ref.dtype)

    @pl.when(pl.program_id(0) >= rope_tiles)
    def _():
        o_ref[...] = acc.astype(o_ref.dtype)


def _mm_rope(x, w, cos, sin, *, col0, n_out, rope_cols, tm, tn, out_dtype, name):
    m, k = x.shape
    pos_tiles = cos.shape[0] // tm
    return pl.pallas_call(
        functools.partial(_mm_rope_kernel, tn=tn, rope_tiles=rope_cols // tn),
        grid=(n_out // tn, m // tm),
        in_specs=[pl.BlockSpec((tm, k), lambda j, i: (i, 0)),
                  _w_spec(w, tn, lambda j, i: col0 // tn + j),
                  pl.BlockSpec((tm, NSA_HEAD_DIM), lambda j, i: (i % pos_tiles, 0)),
                  pl.BlockSpec((tm, NSA_HEAD_DIM), lambda j, i: (i % pos_tiles, 0))],
        out_specs=pl.BlockSpec((tm, tn), lambda j, i: (i, j)),
        out_shape=jax.ShapeDtypeStruct((m, n_out), out_dtype),
        compiler_params=_params("arbitrary", "arbitrary"),
        name=name,
    )(x, w, cos, sin)


def _mm_res_kernel(a_ref, w_ref, res_ref, gate_ref, o_ref):
    o_ref[...] = res_ref[...] + gate_ref[...] * _dot(a_ref[...], w_ref[...])


def _mm_res(a, w, res, mod, kind, gate_chunk, *, tm, tn, rows_per_batch):
    m, k = a.shape
    n = w.shape[-1]
    tn = _row_tile(n, tn)
    swap = lambda f: (lambda j, i: f(i, j))
    gspec = _mod_spec(mod, kind, gate_chunk, tn, tm, rows_per_batch)
    gspec = pl.BlockSpec(gspec.block_shape, swap(gspec.index_map))
    return pl.pallas_call(
        _mm_res_kernel,
        grid=(n // tn, m // tm),
        in_specs=[pl.BlockSpec((tm, k), lambda j, i: (i, 0)),
                  _w_spec(w, tn, lambda j, i: j),
                  pl.BlockSpec((tm, tn), lambda j, i: (i, j)),
                  gspec],
        out_specs=pl.BlockSpec((tm, tn), lambda j, i: (i, j)),
        out_shape=jax.ShapeDtypeStruct((m, n), F32),
        compiler_params=_params("arbitrary", "arbitrary"),
        name="mm_residual",
    )(a, w, res, mod)


def _swiglu_up_kernel(x_ref, wg_ref, wu_ref, o_ref):
    x = x_ref[...]
    o_ref[...] = (_silu(_dot(x, wg_ref[...])) * _dot(x, wu_ref[...])).astype(o_ref.dtype)


def _swiglu_up(x, wg, wu, *, tm, tn):
    m, k = x.shape
    f = wg.shape[-1]
    return pl.pallas_call(
        _swiglu_up_kernel,
        grid=(pl.cdiv(f, tn), m // tm),
        in_specs=[pl.BlockSpec((tm, k), lambda j, i: (i, 0)),
                  _w_spec(wg, tn, lambda j, i: j),
                  _w_spec(wu, tn, lambda j, i: j)],
        out_specs=pl.BlockSpec((tm, tn), lambda j, i: (i, j)),
        out_shape=jax.ShapeDtypeStruct((m, f), BF16),
        compiler_params=_params("arbitrary", "arbitrary"),
        name="swiglu_up",
    )(x, wg, wu)


def _grouped_kernel(meta_ref, *refs, body, n_blk):
    @pl.when(pl.program_id(1) < meta_ref[n_blk])
    def _():
        body(*refs)


def _grouped_specs(n_blk, tm, k_in, tn, weights):
    row = lambda r, meta: jnp.minimum(r, meta[n_blk] - 1)
    x_spec = pl.BlockSpec((tm, k_in), lambda j, r, meta: (row(r, meta), 0))
    w_specs = [pl.BlockSpec((None, None, w.shape[-2], tn), lambda j, r, meta: (0, meta[row(r, meta)], 0, j))
               for w in weights]
    o_spec = pl.BlockSpec((tm, tn), lambda j, r, meta: (row(r, meta), j))
    return [x_spec] + w_specs, o_spec


def _moe_up(xb, meta, wg, wu, *, tm, tn):
    rows, k = xb.shape
    f = wg.shape[-1]
    tn = _row_tile(f, tn)
    n_blk = rows // tm
    in_specs, o_spec = _grouped_specs(n_blk, tm, k, tn, [wg, wu])
    grid_spec = pltpu.PrefetchScalarGridSpec(
        num_scalar_prefetch=1, grid=(f // tn, n_blk), in_specs=in_specs, out_specs=o_spec)
    return pl.pallas_call(
        functools.partial(_grouped_kernel, body=_swiglu_up_kernel, n_blk=n_blk),
        grid_spec=grid_spec,
        out_shape=jax.ShapeDtypeStruct((rows, f), BF16),
        compiler_params=_params("arbitrary", "arbitrary"),
        name="moe_up",
    )(meta, xb, wg, wu)


def _mm_plain_kernel(a_ref, w_ref, o_ref):
    o_ref[...] = _dot(a_ref[...], w_ref[...])


def _moe_down(act, meta, wd, *, tm, tn):
    rows, f = act.shape
    d = wd.shape[-1]
    tn = _row_tile(d, tn)
    n_blk = rows // tm
    in_specs, o_spec = _grouped_specs(n_blk, tm, f, tn, [wd])
    grid_spec = pltpu.PrefetchScalarGridSpec(
        num_scalar_prefetch=1, grid=(d // tn, n_blk), in_specs=in_specs, out_specs=o_spec)
    return pl.pallas_call(
        functools.partial(_grouped_kernel, body=_mm_plain_kernel, n_blk=n_blk),
        grid_spec=grid_spec,
        out_shape=jax.ShapeDtypeStruct((rows, d), F32),
        compiler_params=_params("arbitrary", "arbitrary"),
        name="moe_down",
    )(meta, act, wd)


def _router_kernel(x_ref, w_ref, o_ref):
    lg = _dot(x_ref[...], w_ref[...])
    lane = lax.broadcasted_iota(jnp.int32, lg.shape, 1)
    x = jnp.where(lane < N_EXPERTS, lg, -jnp.inf)
    v0 = jnp.max(x, axis=-1, keepdims=True)
    i0 = jnp.min(jnp.where(x == v0, lane, LANES), axis=-1, keepdims=True)
    x1 = jnp.where(lane == i0, -jnp.inf, x)
    v1 = jnp.max(x1, axis=-1, keepdims=True)
    i1 = jnp.min(jnp.where(x1 == v1, lane, LANES), axis=-1, keepdims=True)
    e1 = jnp.exp(v1 - v0)
    g0 = 1.0 / (1.0 + e1)
    g1 = e1 / (1.0 + e1)
    o_ref[...] = jnp.where(lane == 0, g0,
                           jnp.where(lane == 1, g1,
                                     jnp.where(lane == 2, i0.astype(F32),
                                               jnp.where(lane == 3, i1.astype(F32), 0.0))))


def _router(h, w_router, tm):
    n, d = h.shape
    w_pad = jnp.pad(w_router, ((0, 0), (0, LANES - N_EXPERTS)))
    out = pl.pallas_call(
        _router_kernel,
        grid=(n // tm,),
        in_specs=[pl.BlockSpec((tm, d), lambda i: (i, 0)),
                  pl.BlockSpec((d, LANES), lambda i: (0, 0))],
        out_specs=pl.BlockSpec((tm, LANES), lambda i: (i, 0)),
        out_shape=jax.ShapeDtypeStruct((n, LANES), F32),
        compiler_params=_params("arbitrary"),
        name="moe_router",
    )(h, w_pad)
    return out[:, :TOP_K], out[:, TOP_K:2 * TOP_K].astype(jnp.int32)


def _combine_norm_kernel(x_ref, y0_ref, y1_ref, w_ref, gate_ref, g_ref, o_ref):
    w = w_ref[...]
    y = w[:, 0:1] * y0_ref[...] + w[:, 1:2] * y1_ref[...]
    xo = x_ref[...] + gate_ref[...] * y
    o_ref[...] = _rms(xo) * g_ref[...]


def _combine_norm(x, y0, y1, wts, mod, kind, gate_chunk, g, *, tm, rows_per_batch):
    m, d = x.shape
    row = pl.BlockSpec((tm, d), lambda i, j: (i, 0))
    return pl.pallas_call(
        _combine_norm_kernel,
        grid=(m // tm, 1),
        in_specs=[row, row, row,
                  pl.BlockSpec((tm, TOP_K), lambda i, j: (i, 0)),
                  _mod_spec(mod, kind, gate_chunk, d, tm, rows_per_batch),
                  pl.BlockSpec((1, d), lambda i, j: (0, 0))],
        out_specs=row,
        out_shape=jax.ShapeDtypeStruct((m, d), F32),
        compiler_params=_params("arbitrary", "arbitrary"),
        name="moe_combine_final_norm",
    )(x, y0, y1, wts, mod, g.reshape(1, d))


def _compress_kernel(*refs, n_in):
    w = refs[n_in][...]
    o_ref = refs[n_in + 1]
    outs = []
    for x_ref in refs[:n_in]:
        x = x_ref[...]
        nb = x.shape[0] // NSA_BLOCK
        outs.append(jnp.sum(x.reshape(nb, NSA_BLOCK, x.shape[1]) * w[None], axis=1))
    o_ref[...] = outs[0] if n_in == 1 else jnp.concatenate(outs, axis=0)


def _pool_weights(w_pool):
    w = jnp.transpose(w_pool, (1, 0, 2))[:, :, None, :]
    w = jnp.broadcast_to(w, (NSA_BLOCK, 2, NSA_KV_GROUPS, NSA_HEAD_DIM))
    return w.reshape(NSA_BLOCK, 2 * NSA_KV_WIDTH)


def _compress_prompt(kv, w_full, rows_per_step):
    b, t, kvw = kv.shape
    nb = rows_per_step // NSA_BLOCK
    return pl.pallas_call(
        functools.partial(_compress_kernel, n_in=1),
        grid=(b, t // rows_per_step),
        in_specs=[pl.BlockSpec((None, rows_per_step, kvw), lambda i, s: (i, s, 0)),
                  pl.BlockSpec((NSA_BLOCK, kvw), lambda i, s: (0, 0))],
        out_specs=pl.BlockSpec((None, nb, kvw), lambda i, s: (i, s, 0)),
        out_shape=jax.ShapeDtypeStruct((b, t // NSA_BLOCK, kvw), F32),
        compiler_params=_params("arbitrary", "arbitrary"),
        name="nsa_compress_prompt",
    )(kv, w_full)


KV_SLOTS = 2 * NSA_KV_GROUPS


def _slot_rows(ref, slot, n_tok, first_tok=0):
    return ref[pl.ds(first_tok * KV_SLOTS + slot, n_tok, stride=KV_SLOTS), :]


def _compress_pages_kernel(pt_ref, *refs, n_in):
    del pt_ref
    w_ref, o_ref = refs[n_in], refs[n_in + 1]
    hd = NSA_HEAD_DIM
    bpp = PAGE_SIZE // NSA_BLOCK
    for k, x_ref in enumerate(refs[:n_in]):
        for slot in range(KV_SLOTS):
            x = _slot_rows(x_ref, slot, PAGE_SIZE)
            w = w_ref[slot // NSA_KV_GROUPS]
            pooled = jnp.sum(x.reshape(bpp, NSA_BLOCK, hd) * w[None], axis=1)
            o_ref[k * bpp:(k + 1) * bpp, slot * hd:(slot + 1) * hd] = pooled


def _compress_pages(cache, page_table, w_pool, pages_per_step):
    bsz, n_pages = page_table.shape
    rows, hd = cache.shape[1:]
    p = pages_per_step
    bpp = PAGE_SIZE // NSA_BLOCK
    kvw = KV_SLOTS * hd

    def page_spec(k):
        return pl.BlockSpec((None, rows, hd), lambda i, s, pt: (pt[i, s * p + k], 0, 0))

    grid_spec = pltpu.PrefetchScalarGridSpec(
        num_scalar_prefetch=1,
        grid=(bsz, n_pages // p),
        in_specs=[page_spec(k) for k in range(p)] + [pl.BlockSpec(w_pool.shape, lambda i, s, pt: (0, 0, 0))],
        out_specs=pl.BlockSpec((None, p * bpp, kvw), lambda i, s, pt: (i, s, 0)))
    return pl.pallas_call(
        functools.partial(_compress_pages_kernel, n_in=p),
        grid_spec=grid_spec,
        out_shape=jax.ShapeDtypeStruct((bsz, n_pages * bpp, kvw), F32),
        compiler_params=_params("arbitrary", "arbitrary"),
        name="nsa_compress_pages",
    )(page_table, *([cache] * p), w_pool)


def _select_blocks(score, n_valid):
    rows, width = score.shape
    col = lax.broadcasted_iota(jnp.int32, (rows, width), 1)
    rank = jnp.zeros((rows, width), F32)
    for m in range(n_valid):
        c = score[:, m:m + 1]
        beats = jnp.where(c > score, 1.0, jnp.where((c == score) & (col > m), 1.0, 0.0))
        rank = rank + beats
    return jnp.where((rank < NSA_TOP_BLOCKS) & (col < n_valid), 1.0, 0.0)


def _softmax_rows(s):
    e = jnp.exp(s - jnp.max(s, axis=-1, keepdims=True))
    return e, jnp.sum(e, axis=-1, keepdims=True)


def _compressed_branch(q4, ck, cv, tpos, reps):
    scale = NSA_HEAD_DIM ** -0.5
    nc = ck.shape[0]
    tq = q4.shape[0] // reps
    s = _dot_nt(q4, ck) * scale
    blk = lax.broadcasted_iota(jnp.int32, (1, nc), 1)
    ok = ((blk + 1) * NSA_BLOCK - 1) <= tpos
    e, l = _softmax_rows(jnp.where(ok, s, NEG_INF))
    p = jnp.where(ok, e / l, 0.0)
    o_c = _dot(p, cv)
    imp = p[0:tq]
    for r in range(1, reps):
        imp = imp + p[r * tq:(r + 1) * tq]
    return o_c, imp


def _compressed_select_t(q4, ck, cv, qpos_row, reps):
    scale = NSA_HEAD_DIM ** -0.5
    nc = ck.shape[0]
    tq = q4.shape[0] // reps
    tpos = jnp.concatenate([qpos_row] * reps, axis=1)
    blk = lax.broadcasted_iota(jnp.int32, (nc, 1), 0)
    s = _dot_nt(ck, q4) * scale
    ok = ((blk + 1) * NSA_BLOCK - 1) <= tpos
    s = jnp.where(ok, s, NEG_INF)
    e = jnp.exp(s - jnp.max(s, axis=0, keepdims=True))
    p = jnp.where(ok, e / jnp.sum(e, axis=0, keepdims=True), 0.0)
    o_c = lax.dot_general(p.astype(BF16), cv.astype(BF16), _TN, preferred_element_type=F32)
    imp = p[:, 0:tq]
    for r in range(1, reps):
        imp = imp + p[:, r * tq:(r + 1) * tq]
    cur = qpos_row // NSA_BLOCK
    forced = (blk == 0) | (blk == cur) | (blk == cur - 1)
    score = jnp.where(forced, FORCE_SCORE, jnp.where(blk <= cur, imp, -1.0))
    row = lax.broadcasted_iota(jnp.int32, (nc, tq), 0)
    rank = jnp.zeros((nc, tq), F32)
    for m in range(nc):
        c = score[m:m + 1, :]
        rank = rank + jnp.where(c > score, 1.0, jnp.where((c == score) & (row > m), 1.0, 0.0))
    return o_c, jnp.where(rank < NSA_TOP_BLOCKS, 1.0, 0.0)


def _block_scores(imp, qpos, n_blocks_pad):
    tq, n_imp = imp.shape
    if n_blocks_pad > n_imp:
        imp = jnp.concatenate([imp, jnp.zeros((tq, n_blocks_pad - n_imp), F32)], axis=1)
    blk = lax.broadcasted_iota(jnp.int32, (1, n_blocks_pad), 1)
    cur = qpos // NSA_BLOCK
    forced = (blk == 0) | (blk == cur) | (blk == cur - 1)
    return jnp.where(forced, FORCE_SCORE, jnp.where(blk <= cur, imp, -1.0))


def _nsa_prompt_kernel(q_ref, ck_ref, cv_ref, ks_ref, vs_ref, kw_ref, vw_ref, gate_ref, o_ref, *, tq, kc, seq):
    hd, reps = NSA_HEAD_DIM, NSA_REP
    c2 = (hd ** -0.5) * LOG2_E
    q0 = pl.program_id(2) * tq
    qf = q_ref[...]
    q4 = jnp.concatenate([qf[:, r * hd:(r + 1) * hd] for r in range(reps)], axis=0).astype(BF16)
    qpos = q0 + lax.broadcasted_iota(jnp.int32, (tq, 1), 0)
    m_rows = reps * tq
    stack = lambda x: jnp.concatenate([x] * reps, axis=0)

    n_blocks = seq // NSA_BLOCK
    qpos_row = q0 + lax.broadcasted_iota(jnp.int32, (1, tq), 1)
    o_c, sel = _compressed_select_t(q4, ck_ref[...], cv_ref[...], qpos_row, reps)
    sel = sel.astype(BF16)

    blk_row = lax.broadcasted_iota(jnp.int32, (n_blocks, 1), 0)

    def chunk(c, carry):
        m_i, l_i, acc = carry
        k0 = pl.multiple_of(c * kc, kc)
        kk = ks_ref[pl.ds(k0, kc), :]
        vv = vs_ref[pl.ds(k0, kc), :]
        kpos = k0 + lax.broadcasted_iota(jnp.int32, (1, kc), 1)
        expand = jnp.where(kpos // NSA_BLOCK == blk_row, 1.0, 0.0).astype(BF16)
        selk = lax.dot_general(sel, expand, _TN, preferred_element_type=F32)
        bias = jnp.where(kpos <= qpos, (1.0 - selk) * NEG_INF, NEG_INF)
        s = _dot_nt(q4, kk) * c2 + stack(bias)
        m_new = jnp.maximum(m_i, jnp.max(s, axis=-1, keepdims=True))
        alpha = jnp.exp2(m_i - m_new)
        p = jnp.exp2(s - m_new)
        l_new = alpha * l_i + jnp.sum(p, axis=-1, keepdims=True)
        return m_new, l_new, alpha * acc + _dot(p, vv)

    n_chunks = (q0 + tq + kc - 1) // kc
    init = (jnp.full((m_rows, 1), NEG_INF, F32), jnp.zeros((m_rows, 1), F32), jnp.zeros((m_rows, hd), F32))
    _, l_s, acc_s = lax.fori_loop(0, n_chunks, chunk, init)
    o_s = acc_s / l_s

    span = NSA_WINDOW + tq
    w0 = pl.multiple_of(jnp.maximum(q0 - NSA_WINDOW, 0), tq)
    dist = qpos - (w0 + lax.broadcasted_iota(jnp.int32, (1, span), 1))
    bias = jnp.where(dist >= 0, jnp.where(dist < NSA_WINDOW, 0.0, NEG_INF), NEG_INF)
    s = _dot_nt(q4, kw_ref[pl.ds(w0, span), :]) * c2 + stack(bias)
    e = jnp.exp2(s - jnp.max(s, axis=-1, keepdims=True))
    o_w = _dot(e, vw_ref[pl.ds(w0, span), :]) / jnp.sum(e, axis=-1, keepdims=True)

    gate = gate_ref[...]
    for r in range(reps):
        rows = slice(r * tq, (r + 1) * tq)
        o = (gate[:, 3 * r:3 * r + 1] * o_c[rows] + gate[:, 3 * r + 1:3 * r + 2] * o_s[rows]
             + gate[:, 3 * r + 2:3 * r + 3] * o_w[rows])
        o_ref[:, r * hd:(r + 1) * hd] = o.astype(o_ref.dtype)


def _nsa_prompt_attention(q, ckv, kv_s, kv_w, gates, *, tq, kc):
    b, t, _ = q.shape
    g_, hd = NSA_KV_GROUPS, NSA_HEAD_DIM
    qw = NSA_REP * hd
    nc = ckv.shape[1]
    keys = pl.BlockSpec((None, t, hd), lambda i, g, s: (i, 0, g))
    vals = pl.BlockSpec((None, t, hd), lambda i, g, s: (i, 0, g_ + g))
    return pl.pallas_call(
        functools.partial(_nsa_prompt_kernel, tq=tq, kc=kc, seq=t),
        grid=(b, g_, t // tq),
        in_specs=[pl.BlockSpec((None, tq, qw), lambda i, g, s: (i, s, g)),
                  pl.BlockSpec((None, nc, hd), lambda i, g, s: (i, 0, g)),
                  pl.BlockSpec((None, nc, hd), lambda i, g, s: (i, 0, g_ + g)),
                  keys, vals, keys, vals,
                  pl.BlockSpec((None, tq, LANES), lambda i, g, s: (i, s, g))],
        out_specs=pl.BlockSpec((None, tq, qw), lambda i, g, s: (i, s, g)),
        out_shape=jax.ShapeDtypeStruct((b, t, NSA_Q_WIDTH), BF16),
        compiler_params=_params("arbitrary", "arbitrary", "arbitrary"),
        name="nsa_prompt_attention",
    )(q, ckv, ckv, kv_s, kv_s, kv_w, kv_w, gates)


def _nsa_sample_select_kernel(q_ref, ckv_ref, oc_ref, bias_ref, *, t_new, past, n_blocks, n_pad, chunk):
    hd, reps = NSA_HEAD_DIM, NSA_REP
    qpos = past + lax.broadcasted_iota(jnp.int32, (t_new, 1), 0)
    tpos = jnp.concatenate([qpos] * reps, axis=0)
    sels = []
    for g in range(NSA_KV_GROUPS):
        ck = ckv_ref[:, g * hd:(g + 1) * hd]
        cv = ckv_ref[:, NSA_KV_WIDTH + g * hd:NSA_KV_WIDTH + (g + 1) * hd]
        o_c, imp = _compressed_branch(q_ref[g].astype(BF16), ck, cv, tpos, reps)
        oc_ref[g] = o_c
        sels.append(_select_blocks(_block_scores(imp, qpos, n_pad), n_blocks).astype(BF16))
    n_keys = bias_ref.shape[-1]
    blk_row = lax.broadcasted_iota(jnp.int32, (n_pad, 1), 0)
    for k0 in range(0, n_keys, chunk):
        width = min(chunk, n_keys - k0)
        kpos = k0 + lax.broadcasted_iota(jnp.int32, (1, width), 1)
        expand = jnp.where(kpos // NSA_BLOCK == blk_row, 1.0, 0.0).astype(BF16)
        for g in range(NSA_KV_GROUPS):
            selk = jnp.dot(sels[g], expand, preferred_element_type=F32)
            bias_ref[g, :, k0:k0 + width] = jnp.where(kpos <= qpos, (1.0 - selk) * NEG_INF, NEG_INF)


def _nsa_sample_select(q4, ckv, *, t_new, past, n_blocks, n_pad, n_keys, chunk):
    b = q4.shape[0]
    g_, hd = NSA_KV_GROUPS, NSA_HEAD_DIM
    rows = NSA_REP * t_new
    nc = ckv.shape[1]
    return pl.pallas_call(
        functools.partial(_nsa_sample_select_kernel, t_new=t_new, past=past, n_blocks=n_blocks, n_pad=n_pad,
                          chunk=chunk),
        grid=(b,),
        in_specs=[pl.BlockSpec((None, g_, rows, hd), lambda i: (i, 0, 0, 0)),
                  pl.BlockSpec((None, nc, 2 * NSA_KV_WIDTH), lambda i: (i, 0, 0))],
        out_specs=[pl.BlockSpec((None, g_, rows, hd), lambda i: (i, 0, 0, 0)),
                   pl.BlockSpec((None, g_, t_new, n_keys), lambda i: (i, 0, 0, 0))],
        out_shape=[jax.ShapeDtypeStruct((b, g_, rows, hd), F32),
                   jax.ShapeDtypeStruct((b, g_, t_new, n_keys), F32)],
        compiler_params=_params("arbitrary"),
        name="nsa_sample_select",
    )(q4, ckv)


def _nsa_sample_attend_kernel(pt_ref, *refs, n_pg, t_new, past):
    del pt_ref
    pages = refs[:n_pg]
    (q_ref, bias_ref, bias_new_ref, oc_ref, new_s_ref, win_ref, new_w_ref, gate_ref,
     o_ref, win_out_ref, m_ref, l_ref, acc_ref) = refs[n_pg:]
    hd, reps, g_ = NSA_HEAD_DIM, NSA_REP, NSA_KV_GROUPS
    kvw = NSA_KV_WIDTH
    scale = hd ** -0.5
    c2 = scale * LOG2_E
    step = pl.program_id(1)
    qpos = past + lax.broadcasted_iota(jnp.int32, (t_new, 1), 0)
    tpos = jnp.concatenate([qpos] * reps, axis=0)
    stack = lambda x: jnp.concatenate([x] * reps, axis=0)

    def online(g, s, vv):
        m_i = m_ref[g]
        m_new = jnp.maximum(m_i, jnp.max(s, axis=-1, keepdims=True))
        alpha = jnp.exp2(m_i - m_new)
        p = jnp.exp2(s - m_new)
        l_ref[g] = alpha * l_ref[g] + jnp.sum(p, axis=-1, keepdims=True)
        acc_ref[g] = alpha * acc_ref[g] + _dot(p, vv)
        m_ref[g] = m_new

    def new_rows(ref, c0):
        x = ref[:, c0:c0 + hd]
        return jnp.concatenate([x, jnp.zeros((PAGE_SIZE - t_new, hd), x.dtype)], axis=0)

    @pl.when(step == 0)
    def _():
        m_ref[...] = jnp.full(m_ref.shape, NEG_INF, F32)
        l_ref[...] = jnp.zeros(l_ref.shape, F32)
        acc_ref[...] = jnp.zeros(acc_ref.shape, F32)
        for g in range(g_):
            kk = new_rows(new_s_ref, g * hd)
            vv = new_rows(new_s_ref, kvw + g * hd)
            online(g, _dot_nt(q_ref[g], kk) * c2 + stack(bias_new_ref[g]), vv)

    for g in range(g_):
        kk = jnp.concatenate([_slot_rows(pg, g, PAGE_SIZE).astype(BF16) for pg in pages], axis=0)
        vv = jnp.concatenate([_slot_rows(pg, g_ + g, PAGE_SIZE).astype(BF16) for pg in pages], axis=0)
        online(g, _dot_nt(q_ref[g], kk) * c2 + stack(bias_ref[g]), vv)

    @pl.when(step == pl.num_programs(1) - 1)
    def _():
        wb = win_ref.shape[0] // KV_SLOTS
        wpos = past - wb + lax.broadcasted_iota(jnp.int32, (1, wb + PAGE_SIZE), 1)
        dist = tpos - wpos
        gate = gate_ref[...]
        for g in range(g_):
            kk = jnp.concatenate([_slot_rows(win_ref, g, wb), new_rows(new_w_ref, g * hd)], axis=0)
            vv = jnp.concatenate([_slot_rows(win_ref, g_ + g, wb), new_rows(new_w_ref, kvw + g * hd)], axis=0)
            s = _dot_nt(q_ref[g], kk) * scale
            ok = (wpos >= 0) & (dist >= 0) & (dist < NSA_WINDOW)
            e, l = _softmax_rows(jnp.where(ok, s, NEG_INF))
            o_w = _dot(e, vv) / l
            o_s = acc_ref[g] / l_ref[g]
            o_c = oc_ref[g]
            for r in range(reps):
                rs = slice(r * t_new, (r + 1) * t_new)
                c0 = g * LANES + 3 * r
                o = (gate[:, c0:c0 + 1] * o_c[rs] + gate[:, c0 + 1:c0 + 2] * o_s[rs]
                     + gate[:, c0 + 2:c0 + 3] * o_w[rs])
                col = (g * reps + r) * hd
                o_ref[:, col:col + hd] = o.astype(o_ref.dtype)
        keep = (wb - t_new) * KV_SLOTS
        win_out_ref[0:keep, :] = win_ref[t_new * KV_SLOTS:wb * KV_SLOTS, :]
        for slot in range(KV_SLOTS):
            win_out_ref[pl.ds(keep + slot, t_new, stride=KV_SLOTS), :] = new_w_ref[:, slot * hd:(slot + 1) * hd]


def _nsa_sample_attend(q4, bias, o_c, kv_s, win, kv_w, gates, cache, page_table, *, n_pg, past):
    b, g_, rows, hd = q4.shape
    t_new = rows // NSA_REP
    width = n_pg * PAGE_SIZE
    kvw2 = 2 * NSA_KV_WIDTH
    n_pages = page_table.shape[1]
    win_rows = win.shape[1]
    page_rows = cache.shape[1]

    def page_spec(k):
        return pl.BlockSpec((None, page_rows, hd), lambda i, s, pt: (pt[i, s * n_pg + k], 0, 0))

    per_b = lambda shape: pl.BlockSpec((None,) + shape, lambda i, s, pt: (i,) + (0,) * len(shape))
    grid_spec = pltpu.PrefetchScalarGridSpec(
        num_scalar_prefetch=1,
        grid=(b, n_pages // n_pg),
        in_specs=[page_spec(k) for k in range(n_pg)] + [
            per_b((g_, rows, hd)),
            pl.BlockSpec((None, g_, t_new, width), lambda i, s, pt: (i, 0, 0, s)),
            pl.BlockSpec((None, g_, t_new, PAGE_SIZE), lambda i, s, pt: (i, 0, 0, past // PAGE_SIZE)),
            per_b((g_, rows, hd)),
            per_b((t_new, kvw2)), per_b((win_rows, hd)), per_b((t_new, kvw2)),
            per_b((t_new, g_ * LANES))],
        out_specs=[per_b((t_new, NSA_Q_WIDTH)), per_b((win_rows, hd))],
        scratch_shapes=[pltpu.VMEM((g_, rows, 1), F32), pltpu.VMEM((g_, rows, 1), F32),
                        pltpu.VMEM((g_, rows, hd), F32)])
    return pl.pallas_call(
        functools.partial(_nsa_sample_attend_kernel, n_pg=n_pg, t_new=t_new, past=past),
        grid_spec=grid_spec,
        out_shape=[jax.ShapeDtypeStruct((b, t_new, NSA_Q_WIDTH), BF16),
                   jax.ShapeDtypeStruct((b, win_rows, hd), F32)],
        compiler_params=_params("arbitrary", "arbitrary"),
        name="nsa_sample_attention",
    )(page_table, *([cache] * n_pg), q4, bias, bias, o_c, kv_s, win, kv_w, gates)


def _log_sigmoid(x):
    return jnp.minimum(x, 0.0) - jnp.log1p(jnp.exp(-jnp.abs(x)))


def _mlstm_kernel(q_ref, k_ref, v_ref, o_ref, gc_ref, gr_ref, bc_ref, br_ref, ng_ref,
                  c0_ref, n0_ref, m0_ref, y_ref, c_out, n_out, m_out, c_s, n_s, m_s):
    nh, dk, dv = MLSTM_HEADS, MLSTM_QK_DIM, MLSTM_V_DIM
    step = pl.program_id(1)
    ln = q_ref.shape[0]

    @pl.when(step == 0)
    def _():
        c_s[...] = c0_ref[...]
        n_s[...] = n0_ref[...]
        m_s[...] = m0_ref[...]

    hi = lax.Precision.HIGHEST
    r_i = lax.broadcasted_iota(jnp.int32, (ln, ln), 0)
    c_i = lax.broadcasted_iota(jnp.int32, (ln, ln), 1)
    causal = c_i <= r_i
    lower = jnp.where(causal, 1.0, 0.0)
    upper = jnp.where(r_i <= c_i, 1.0, 0.0)
    gcol = gc_ref[...] + bc_ref[...]
    grow = gr_ref[...] + br_ref[...]
    ig_c = gcol[:, :nh]
    b_c = jnp.dot(lower, _log_sigmoid(gcol[:, nh:]), precision=hi, preferred_element_type=F32)
    ig_r = grow[:nh, :]
    b_r = jnp.dot(_log_sigmoid(grow[nh:, :]), upper, precision=hi, preferred_element_type=F32)

    for h in range(nh):
        q = q_ref[:, h * dk:(h + 1) * dk]
        k = k_ref[:, h * dk:(h + 1) * dk] * (dk ** -0.5)
        v = v_ref[:, h * dv:(h + 1) * dv]
        qb, kb, vb = q.astype(BF16), k.astype(BF16), v.astype(BF16)
        bt = b_c[:, h:h + 1]
        b_last = b_c[ln - 1:ln, h:h + 1]
        m_prev = m_s[h]
        d = jnp.where(causal, bt - b_r[h:h + 1, :] + ig_r[h:h + 1, :], -jnp.inf)
        inter = bt + m_prev
        m_t = jnp.maximum(inter, jnp.max(d, axis=-1, keepdims=True))
        w = jnp.exp(d - m_t) * _dot_nt(qb, kb)
        a = jnp.exp(inter - m_t)
        c_prev = c_s[h]
        n_prev = n_s[h]
        num = a * _dot(qb, c_prev) + _dot(w, vb)
        den = a * jnp.sum(q * n_prev, axis=-1, keepdims=True) + jnp.sum(w, axis=-1, keepdims=True)
        hh = num / jnp.maximum(jnp.abs(den), jnp.exp(-m_t))
        m_new = m_t[ln - 1:ln, :]
        ws = jnp.exp(b_last - bt + ig_c[:, h:h + 1] - m_new)
        decay = jnp.exp(b_last + m_prev - m_new)
        kw = k * ws
        c_s[h] = decay * c_prev + lax.dot_general(kw.astype(BF16), vb, _TN, preferred_element_type=F32)
        n_s[h] = decay * n_prev + jnp.sum(kw, axis=0, keepdims=True)
        m_s[h] = m_new
        hn = _rms(hh) * ng_ref[:, h * dv:(h + 1) * dv]
        y_ref[:, h * dv:(h + 1) * dv] = (hn * _sigmoid(o_ref[:, h * dv:(h + 1) * dv])).astype(y_ref.dtype)

    @pl.when(step == pl.num_programs(1) - 1)
    def _():
        c_out[...] = c_s[...]
        n_out[...] = n_s[...]
        m_out[...] = m_s[...]


def _mlstm(proj, graw, b_gate, norm_g, c0, n0, m0, *, chunk):
    b, t, _ = proj.shape
    nh, dk, dv = MLSTM_HEADS, MLSTM_QK_DIM, MLSTM_V_DIM
    nchunk = t // chunk
    qw, vw = nh * dk, nh * dv
    grow = jnp.transpose(graw.reshape(b, nchunk, chunk, 2 * nh), (0, 1, 3, 2))
    state = lambda shape: pl.BlockSpec((None,) + shape, lambda i, s: (i,) + (0,) * len(shape))
    const = lambda shape: pl.BlockSpec(shape, lambda i, s: (0,) * len(shape))
    return pl.pallas_call(
        _mlstm_kernel,
        grid=(b, nchunk),
        in_specs=[pl.BlockSpec((None, chunk, qw), lambda i, s: (i, s, 0)),
                  pl.BlockSpec((None, chunk, qw), lambda i, s: (i, s, 1)),
                  pl.BlockSpec((None, chunk, vw), lambda i, s: (i, s, 2 * qw // vw)),
                  pl.BlockSpec((None, chunk, vw), lambda i, s: (i, s, 2 * qw // vw + 1)),
                  pl.BlockSpec((None, chunk, 2 * nh), lambda i, s: (i, s, 0)),
                  pl.BlockSpec((None, None, 2 * nh, chunk), lambda i, s: (i, s, 0, 0)),
                  const((1, 2 * nh)), const((2 * nh, 1)), const((1, vw)),
                  state((nh, dk, dv)), state((nh, 1, dk)), state((nh, 1, 1))],
        out_specs=[pl.BlockSpec((None, chunk, vw), lambda i, s: (i, s, 0)),
                   state((nh, dk, dv)), state((nh, 1, dk)), state((nh, 1, 1))],
        out_shape=[jax.ShapeDtypeStruct((b, t, vw), BF16),
                   jax.ShapeDtypeStruct((b, nh, dk, dv), F32),
                   jax.ShapeDtypeStruct((b, nh, 1, dk), F32),
                   jax.ShapeDtypeStruct((b, nh, 1, 1), F32)],
        scratch_shapes=[pltpu.VMEM((nh, dk, dv), F32), pltpu.VMEM((nh, 1, dk), F32),
                        pltpu.VMEM((nh, 1, 1), F32)],
        compiler_params=_params("arbitrary", "arbitrary"),
        name="mlstm",
    )(proj, proj, proj, proj, graw, grow, b_gate.reshape(1, 2 * nh), b_gate.reshape(2 * nh, 1),
      norm_g.reshape(1, vw), c0, n0.reshape(b, nh, 1, dk), m0.reshape(b, nh, 1, 1))


def _rope_tables(pos):
    half = NSA_HEAD_DIM // 2
    freq = ROPE_THETA ** (-jnp.arange(half, dtype=F32) / half)
    ang = pos.astype(F32)[:, None] * freq[None, :]
    cos, sin = jnp.cos(ang), jnp.sin(ang)
    return jnp.concatenate([cos, cos], axis=-1), jnp.concatenate([-sin, sin], axis=-1)


def _gate_weights(w_in):
    d = w_in.shape[0]
    per_group = NSA_REP * 3
    w = w_in[:, NSA_Q_WIDTH + 6 * NSA_KV_WIDTH:].reshape(d, NSA_KV_GROUPS, per_group)
    w = jnp.pad(w, ((0, 0), (0, 0), (0, LANES - per_group)))
    return w.reshape(d, NSA_KV_GROUPS * LANES)


def _row_tile(m, pref):
    t = min(pref, m)
    while m % t:
        t //= 2
    return t


class _Group:
    def __init__(self, x, mod_rows, kind):
        self.b, self.t, self.d = x.shape
        self.x = x.reshape(self.b * self.t, self.d)
        self.kind = kind
        self.mod_rows = mod_rows

    def mod(self, layer):
        m = self.mod_rows[layer]
        if self.kind == "batch":
            return m.reshape(self.b, 1, m.shape[-1])
        return jnp.repeat(m, self.t, axis=0)


def kernel(x_prompt, x_sample, c_prompt, c_sample, cache_nsa_cmp_kv, cache_nsa_sel_kv, page_table, state_nsa_win_kv, state_mlstm_c, state_mlstm_n, state_mlstm_m, ada_w, ada_b, norm_mix_g, norm_ffn_g, norm_final_g, nsa_w_in, nsa_w_pool, nsa_w_out, mlstm_w_in, mlstm_b_gate, mlstm_norm_g, mlstm_w_out, ffn_w_gate, ffn_w_up, ffn_w_down, moe_w_router, moe_w_gate, moe_w_up, moe_w_down):
    bp, tp, d = x_prompt.shape
    bs, ts, _ = x_sample.shape
    past = page_table.shape[1] * PAGE_SIZE
    assert tp % NSA_BLOCK == 0 and tp >= NSA_WINDOW + 128 and ts < NSA_BLOCK and ts % SUBLANES == 0
    assert past % NSA_BLOCK == 0 and state_nsa_win_kv.shape[2] == NSA_WINDOW
    assert ada_w.shape[0] == 2 and nsa_w_in.shape[0] == 1 and mlstm_w_in.shape[0] == 1

    mod_all = _adaln(jnp.concatenate([c_prompt, c_sample], axis=0), ada_w, ada_b)
    groups = [_Group(x_prompt, mod_all[:, :bp], "batch"), _Group(x_sample, mod_all[:, bp:], "row")]
    tms = [_row_tile(tp, 1024), bs * ts]
    tes = [_row_tile(tp, 512), bs * ts]

    w_gates = _gate_weights(nsa_w_in[0])
    w_full = _pool_weights(nsa_w_pool[0])
    g_, hd = NSA_KV_GROUPS, NSA_HEAD_DIM
    kvw2 = 2 * NSA_KV_WIDTH
    outs = {}
    for gi, (grp, tm, te) in enumerate(zip(groups, tms, tes)):
        b, t = grp.b, grp.t
        mod = grp.mod(0)
        h = _modulate(grp.x, norm_mix_g[0], mod, grp.kind, 0, 1, te, t)
        pos = jnp.arange(t) if gi == 0 else past + jnp.arange(t)
        cos, sin = _rope_tables(pos)
        if gi == 1:
            cos, sin = jnp.tile(cos, (b, 1)), jnp.tile(sin, (b, 1))
        q = _mm_rope(h, nsa_w_in, cos, sin, col0=0, n_out=NSA_Q_WIDTH, rope_cols=NSA_Q_WIDTH,
                     tm=tm, tn=512, out_dtype=BF16, name="nsa_q_proj").reshape(b, t, NSA_Q_WIDTH)
        kv_c, kv_s, kv_w = [
            _mm_rope(h, nsa_w_in, cos, sin, col0=NSA_Q_WIDTH + br * kvw2, n_out=kvw2, rope_cols=NSA_KV_WIDTH,
                     tm=tm, tn=512, out_dtype=F32, name="nsa_kv_proj").reshape(b, t, kvw2)
            for br in range(3)]
        gates = _mm(h, w_gates, tm=tm, tn=g_ * LANES, act="sigmoid", name="nsa_gates").reshape(b, t, g_ * LANES)
        if gi == 0:
            ckv = _compress_prompt(kv_c, w_full, 8 * NSA_BLOCK)
            att = _nsa_prompt_attention(q, ckv, kv_s, kv_w, gates, tq=128, kc=512)
            win_out = kv_w[:, t - min(NSA_WINDOW, t):]
        else:
            n_phys = cache_nsa_cmp_kv.shape[1]
            native = lambda a, lead: a.reshape(lead, -1, hd)
            ckv = _compress_pages(native(cache_nsa_cmp_kv, n_phys), page_table, nsa_w_pool[0],
                                  _row_tile(page_table.shape[1], 16))
            n_blocks = past // NSA_BLOCK + 1
            n_pad = -(-n_blocks // LANES) * LANES
            q4 = jnp.transpose(q.reshape(b, t, g_, NSA_REP, hd), (0, 2, 3, 1, 4)).reshape(b, g_, NSA_REP * t, hd)
            n_pg = 8
            o_c, bias = _nsa_sample_select(q4, ckv, t_new=t, past=past, n_blocks=n_blocks, n_pad=n_pad,
                                           n_keys=past + PAGE_SIZE, chunk=n_pg * PAGE_SIZE)
            att, win_out = _nsa_sample_attend(
                q4, bias, o_c, kv_s, native(state_nsa_win_kv, b), kv_w, gates,
                native(cache_nsa_sel_kv, n_phys), page_table, n_pg=n_pg, past=past)
        as_cache = lambda a: a.reshape(1, b, -1, 2, g_, hd)
        outs[gi] = (as_cache(kv_c), as_cache(kv_s), as_cache(win_out))
        x1 = _mm_res(att.reshape(b * t, NSA_Q_WIDTH), nsa_w_out, grp.x, mod, grp.kind, 2,
                     tm=tm, tn=512, rows_per_batch=t)
        h2 = _modulate(x1, norm_ffn_g[0], mod, grp.kind, 3, 4, te, t)
        act = _swiglu_up(h2, ffn_w_gate, ffn_w_up, tm=tm, tn=512)
        grp.x = _mm_res(act, ffn_w_down, x1, mod, grp.kind, 5, tm=te, tn=512, rows_per_batch=t)

    nh, dk, dv = MLSTM_HEADS, MLSTM_QK_DIM, MLSTM_V_DIM
    n_main = 2 * nh * dk + 2 * nh * dv
    w_g = jnp.pad(mlstm_w_in[0][:, n_main:], ((0, 0), (0, LANES - 2 * nh)))
    states = {}
    hs2, mods = [], []
    for gi, (grp, tm, te) in enumerate(zip(groups, tms, tes)):
        b, t = grp.b, grp.t
        mod = grp.mod(1)
        h = _modulate(grp.x, norm_mix_g[1], mod, grp.kind, 0, 1, te, t)
        proj = _mm(h, mlstm_w_in, n_out=n_main, tm=tm, tn=512, name="mlstm_in_proj").reshape(b, t, n_main)
        graw = _mm(h, w_g, tm=tm, tn=LANES, name="mlstm_gates")[:, :2 * nh].reshape(b, t, 2 * nh)
        if gi == 0:
            c0 = jnp.zeros((b, nh, dk, dv), F32)
            n0 = jnp.zeros((b, nh, dk), F32)
            m0 = jnp.zeros((b, nh), F32)
            chunk = MLSTM_CHUNK
        else:
            c0, n0, m0 = state_mlstm_c[0], state_mlstm_n[0], state_mlstm_m[0]
            chunk = t
        y, c_new, n_new, m_new = _mlstm(proj, graw, mlstm_b_gate[0], mlstm_norm_g[0], c0, n0, m0, chunk=chunk)
        states[gi] = (c_new[None], n_new.reshape(1, b, nh, dk), m_new.reshape(1, b, nh))
        grp.x = _mm_res(y.reshape(b * t, nh * dv), mlstm_w_out, grp.x, mod, grp.kind, 2,
                        tm=tm, tn=512, rows_per_batch=t)
        hs2.append(_modulate(grp.x, norm_ffn_g[1], mod, grp.kind, 3, 4, te, t))
        mods.append(mod)

    h_all = jnp.concatenate(hs2, axis=0)
    n_tok = h_all.shape[0]
    tm_moe = MOE_ROW_BLOCK
    wts, top_e = _router(h_all, moe_w_router[0], _row_tile(n_tok, 256))
    n_asg = n_tok * TOP_K
    flat_e = top_e.reshape(n_asg)
    onehot = (flat_e[:, None] == jnp.arange(N_EXPERTS)[None, :]).astype(jnp.int32)
    within = jnp.take_along_axis(jnp.cumsum(onehot, axis=0), flat_e[:, None], axis=1)[:, 0] - 1
    counts = jnp.sum(onehot, axis=0)
    padded = (counts + tm_moe - 1) // tm_moe * tm_moe
    p_end = jnp.cumsum(padded)
    dest = (p_end - padded)[flat_e] + within
    n_blk = -(-(n_asg + N_EXPERTS * (tm_moe - 1)) // tm_moe)
    n_rows = n_blk * tm_moe
    row_tok = jnp.full((n_rows,), n_tok, jnp.int32).at[dest].set(jnp.arange(n_asg, dtype=jnp.int32) // TOP_K)
    blk_start = jnp.arange(n_blk, dtype=jnp.int32) * tm_moe
    blk_e = jnp.minimum(jnp.sum((p_end[None, :] <= blk_start[:, None]).astype(jnp.int32), axis=1), N_EXPERTS - 1)
    meta = jnp.concatenate([blk_e, p_end[-1:] // tm_moe]).astype(jnp.int32)
    xb = jnp.concatenate([h_all, jnp.zeros((1, d), h_all.dtype)], axis=0)[row_tok]
    act = _moe_up(xb, meta, moe_w_gate, moe_w_up, tm=tm_moe, tn=1024)
    yb = _moe_down(act, meta, moe_w_down, tm=tm_moe, tn=512)
    dest2 = dest.reshape(n_tok, TOP_K)

    finals = []
    off = 0
    for gi, (grp, te) in enumerate(zip(groups, tes)):
        m = grp.b * grp.t
        dg = dest2[off:off + m]
        fin = _combine_norm(grp.x, yb[dg[:, 0]], yb[dg[:, 1]], wts[off:off + m], mods[gi], grp.kind, 5,
                            norm_final_g, tm=te, rows_per_batch=grp.t)
        finals.append(fin.reshape(grp.b, grp.t, d))
        off += m

    return (finals[0], finals[1],
            outs[0][0], outs[0][1], outs[0][2], states[0][0], states[0][1], states[0][2],
            outs[1][0], outs[1][1], outs[1][2], states[1][0], states[1][1], states[1][2])
```

```python
import functools

import jax
import jax.numpy as jnp
from jax import lax
from jax.experimental import pallas as pl
from jax.experimental.pallas import tpu as pltpu

F32 = jnp.float32
BF16 = jnp.bfloat16

PAGE_SIZE = 128
NSA_HEADS = 16
NSA_HEAD_DIM = 128
NSA_KV_GROUPS = 4
NSA_REP = NSA_HEADS // NSA_KV_GROUPS
NSA_BLOCK = 64
NSA_TOP_BLOCKS = 16
NSA_WINDOW = 512
NSA_Q_WIDTH = NSA_HEADS * NSA_HEAD_DIM
NSA_KV_WIDTH = NSA_KV_GROUPS * NSA_HEAD_DIM
ROPE_THETA = 10000.0
FORCE_SCORE = 1.0e4
NEG_INF = -1.0e30
MLSTM_HEADS = 8
MLSTM_QK_DIM = 128
MLSTM_V_DIM = 256
MLSTM_CHUNK = 64
N_EXPERTS = 8
TOP_K = 2
MOE_ROW_BLOCK = 512
NORM_EPS = 1e-6
LOG2_E = 1.4426950408889634

LANES = 128
SUBLANES = 8
VMEM_LIMIT_BYTES = 52 * 1024 * 1024

_NT = (((1,), (1,)), ((), ()))
_TN = (((0,), (0,)), ((), ()))


def _params(*sem):
    return pltpu.CompilerParams(dimension_semantics=sem, vmem_limit_bytes=VMEM_LIMIT_BYTES)


def _dot(a, b):
    return jnp.dot(a.astype(BF16), b.astype(BF16), preferred_element_type=F32)


def _dot_nt(a, b):
    return lax.dot_general(a.astype(BF16), b.astype(BF16), _NT, preferred_element_type=F32)


def _sigmoid(x):
    return 1.0 / (1.0 + jnp.exp(-x))


def _silu(x):
    return x * _sigmoid(x)


def _adaln_kernel(c_ref, w_ref, b_ref, o_ref):
    o_ref[...] = _dot(_silu(c_ref[...]), w_ref[...]) + b_ref[...]


def _adaln(c_all, ada_w, ada_b):
    depth, d, n = ada_w.shape
    tn = _row_tile(n, 1024)
    rows = c_all.shape[0]
    return pl.pallas_call(
        _adaln_kernel,
        grid=(depth, n // tn),
        in_specs=[pl.BlockSpec((rows, d), lambda l, j: (0, 0)),
                  pl.BlockSpec((None, d, tn), lambda l, j: (l, 0, j)),
                  pl.BlockSpec((None, 1, tn), lambda l, j: (l, 0, j))],
        out_specs=pl.BlockSpec((None, rows, tn), lambda l, j: (l, 0, j)),
        out_shape=jax.ShapeDtypeStruct((depth, rows, n), F32),
        compiler_params=_params("arbitrary", "arbitrary"),
        name="adaln",
    )(c_all, ada_w, ada_b.reshape(depth, 1, n))


def _rms(x):
    return x * lax.rsqrt(jnp.mean(x * x, axis=-1, keepdims=True) + NORM_EPS)


def _modulate_kernel(x_ref, g_ref, shift_ref, scale_ref, o_ref):
    y = _rms(x_ref[...]) * g_ref[...]
    o_ref[...] = (y * (1.0 + scale_ref[...]) + shift_ref[...]).astype(o_ref.dtype)


def _mod_spec(mod, kind, chunk, width, tm, rows_per_batch):
    d6 = mod.shape[-1]
    per_chunk = (d6 // 6) // width
    if kind == "batch":
        return pl.BlockSpec((None, 1, width),
                            lambda i, j: ((i * tm) // rows_per_batch, 0, chunk * per_chunk + j))
    return pl.BlockSpec((tm, width), lambda i, j: (i, chunk * per_chunk + j))


def _modulate(x, g, mod, kind, shift_chunk, scale_chunk, tm, rows_per_batch):
    m, d = x.shape
    return pl.pallas_call(
        _modulate_kernel,
        grid=(m // tm, 1),
        in_specs=[pl.BlockSpec((tm, d), lambda i, j: (i, 0)),
                  pl.BlockSpec((1, d), lambda i, j: (0, 0)),
                  _mod_spec(mod, kind, shift_chunk, d, tm, rows_per_batch),
                  _mod_spec(mod, kind, scale_chunk, d, tm, rows_per_batch)],
        out_specs=pl.BlockSpec((tm, d), lambda i, j: (i, 0)),
        out_shape=jax.ShapeDtypeStruct((m, d), BF16),
        compiler_params=_params("arbitrary", "arbitrary"),
        name="modulate",
    )(x, g.reshape(1, d), mod, mod)


def _mm_kernel(x_ref, w_ref, o_ref, *, act):
    acc = _dot(x_ref[...], w_ref[...])
    if act == "sigmoid":
        acc = _sigmoid(acc)
    o_ref[...] = acc.astype(o_ref.dtype)


def _w_spec(w, tn, col_of):
    lead = w.ndim - 2
    assert all(s == 1 for s in w.shape[:lead])
    return pl.BlockSpec((None,) * lead + (w.shape[-2], tn), lambda *idx: (0,) * lead + (0, col_of(*idx)))


def _mm(x, w, *, col0=0, n_out=None, tm, tn, act=None, out_dtype=F32, name="mm"):
    m, k = x.shape
    n = w.shape[-1] if n_out is None else n_out
    return pl.pallas_call(
        functools.partial(_mm_kernel, act=act),
        grid=(n // tn, m // tm),
        in_specs=[pl.BlockSpec((tm, k), lambda j, i: (i, 0)),
                  _w_spec(w, tn, lambda j, i: col0 // tn + j)],
        out_specs=pl.BlockSpec((tm, tn), lambda j, i: (i, j)),
        out_shape=jax.ShapeDtypeStruct((m, n), out_dtype),
        compiler_params=_params("arbitrary", "arbitrary"),
        name=name,
    )(x, w)


def _mm_rope_kernel(x_ref, w_ref, cos_ref, sin_ref, o_ref, *, tn, rope_tiles):
    acc = _dot(x_ref[...], w_ref[...])

    @pl.when(pl.program_id(0) < rope_tiles)
    def _():
        cos = cos_ref[...]
        sin = sin_ref[...]
        for s in range(tn // NSA_HEAD_DIM):
            sl = slice(s * NSA_HEAD_DIM, (s + 1) * NSA_HEAD_DIM)
            a = acc[:, sl]
            o_ref[:, sl] = (a * cos + pltpu.roll(a, NSA_HEAD_DIM // 2, 1) * sin).astype(o_ref.dtype)

    @pl.when(pl.program_id(0) >= rope_tiles)
    def _():
        o_ref[...] = acc.astype(o_ref.dtype)


def _mm_rope(x, w, cos, sin, *, col0, n_out, rope_cols, tm, tn, out_dtype, name):
    m, k = x.shape
    pos_tiles = cos.shape[0] // tm
    return pl.pallas_call(
        functools.partial(_mm_rope_kernel, tn=tn, rope_tiles=rope_cols // tn),
        grid=(n_out // tn, m // tm),
        in_specs=[pl.BlockSpec((tm, k), lambda j, i: (i, 0)),
                  _w_spec(w, tn, lambda j, i: col0 // tn + j),
                  pl.BlockSpec((tm, NSA_HEAD_DIM), lambda j, i: (i % pos_tiles, 0)),
                  pl.BlockSpec((tm, NSA_HEAD_DIM), lambda j, i: (i % pos_tiles, 0))],
        out_specs=pl.BlockSpec((tm, tn), lambda j, i: (i, j)),
        out_shape=jax.ShapeDtypeStruct((m, n_out), out_dtype),
        compiler_params=_params("arbitrary", "arbitrary"),
        name=name,
    )(x, w, cos, sin)


def _kv_proj_kernel(x_ref, w_ref, cos_ref, sin_ref, *o_refs, row_major):
    acc = _dot(x_ref[...], w_ref[...])
    cos = cos_ref[...]
    sin = sin_ref[...]
    hd = NSA_HEAD_DIM
    tm = acc.shape[0]
    for slot in range(KV_SLOTS):
        a = acc[:, slot * hd:(slot + 1) * hd]
        if slot < NSA_KV_GROUPS:
            a = a * cos + pltpu.roll(a, hd // 2, 1) * sin
        o_refs[0][pl.ds(slot, tm, stride=KV_SLOTS), :] = a
        if row_major:
            o_refs[1][:, slot * hd:(slot + 1) * hd] = a


def _kv_proj(x, w, cos, sin, *, col0, tm, row_major, name):
    m, k = x.shape
    hd = NSA_HEAD_DIM
    kvw2 = KV_SLOTS * hd
    pos_tiles = cos.shape[0] // tm
    out_specs = [pl.BlockSpec((tm * KV_SLOTS, hd), lambda i: (i, 0))]
    out_shape = [jax.ShapeDtypeStruct((m * KV_SLOTS, hd), F32)]
    if row_major:
        out_specs.append(pl.BlockSpec((tm, kvw2), lambda i: (i, 0)))
        out_shape.append(jax.ShapeDtypeStruct((m, kvw2), F32))
    return pl.pallas_call(
        functools.partial(_kv_proj_kernel, row_major=row_major),
        grid=(m // tm,),
        in_specs=[pl.BlockSpec((tm, k), lambda i: (i, 0)),
                  _w_spec(w, kvw2, lambda i: col0 // kvw2),
                  pl.BlockSpec((tm, hd), lambda i: (i % pos_tiles, 0)),
                  pl.BlockSpec((tm, hd), lambda i: (i % pos_tiles, 0))],
        out_specs=out_specs,
        out_shape=out_shape,
        compiler_params=_params("arbitrary"),
        name=name,
    )(x, w, cos, sin)


def _mm_res_kernel(a_ref, w_ref, res_ref, gate_ref, o_ref):
    o_ref[...] = res_ref[...] + gate_ref[...] * _dot(a_ref[...], w_ref[...])


def _mm_res(a, w, res, mod, kind, gate_chunk, *, tm, tn, rows_per_batch):
    m, k = a.shape
    n = w.shape[-1]
    tn = _row_tile(n, tn)
    swap = lambda f: (lambda j, i: f(i, j))
    gspec = _mod_spec(mod, kind, gate_chunk, tn, tm, rows_per_batch)
    gspec = pl.BlockSpec(gspec.block_shape, swap(gspec.index_map))
    return pl.pallas_call(
        _mm_res_kernel,
        grid=(n // tn, m // tm),
        in_specs=[pl.BlockSpec((tm, k), lambda j, i: (i, 0)),
                  _w_spec(w, tn, lambda j, i: j),
                  pl.BlockSpec((tm, tn), lambda j, i: (i, j)),
                  gspec],
        out_specs=pl.BlockSpec((tm, tn), lambda j, i: (i, j)),
        out_shape=jax.ShapeDtypeStruct((m, n), F32),
        compiler_params=_params("arbitrary", "arbitrary"),
        name="mm_residual",
    )(a, w, res, mod)


def _swiglu_up_kernel(x_ref, wg_ref, wu_ref, o_ref):
    x = x_ref[...]
    o_ref[...] = (_silu(_dot(x, wg_ref[...])) * _dot(x, wu_ref[...])).astype(o_ref.dtype)


def _swiglu_up(x, wg, wu, *, tm, tn):
    m, k = x.shape
    f = wg.shape[-1]
    return pl.pallas_call(
        _swiglu_up_kernel,
        grid=(pl.cdiv(f, tn), m // tm),
        in_specs=[pl.BlockSpec((tm, k), lambda j, i: (i, 0)),
                  _w_spec(wg, tn, lambda j, i: j),
                  _w_spec(wu, tn, lambda j, i: j)],
        out_specs=pl.BlockSpec((tm, tn), lambda j, i: (i, j)),
        out_shape=jax.ShapeDtypeStruct((m, f), BF16),
        compiler_params=_params("arbitrary", "arbitrary"),
        name="swiglu_up",
    )(x, wg, wu)


def _grouped_kernel(meta_ref, *refs, body, n_blk):
    @pl.when(pl.program_id(1) < meta_ref[n_blk])
    def _():
        body(*refs)


def _grouped_specs(n_blk, tm, k_in, tn, weights):
    row = lambda r, meta: jnp.minimum(r, meta[n_blk] - 1)
    x_spec = pl.BlockSpec((tm, k_in), lambda j, r, meta: (row(r, meta), 0))
    w_specs = [pl.BlockSpec((None, None, w.shape[-2], tn), lambda j, r, meta: (0, meta[row(r, meta)], 0, j))
               for w in weights]
    o_spec = pl.BlockSpec((tm, tn), lambda j, r, meta: (row(r, meta), j))
    return [x_spec] + w_specs, o_spec


def _moe_up(xb, meta, wg, wu, *, tm, tn):
    rows, k = xb.shape
    f = wg.shape[-1]
    tn = _row_tile(f, tn)
    n_blk = rows // tm
    in_specs, o_spec = _grouped_specs(n_blk, tm, k, tn, [wg, wu])
    grid_spec = pltpu.PrefetchScalarGridSpec(
        num_scalar_prefetch=1, grid=(f // tn, n_blk), in_specs=in_specs, out_specs=o_spec)
    return pl.pallas_call(
        functools.partial(_grouped_kernel, body=_swiglu_up_kernel, n_blk=n_blk),
        grid_spec=grid_spec,
        out_shape=jax.ShapeDtypeStruct((rows, f), BF16),
        compiler_params=_params("arbitrary", "arbitrary"),
        name="moe_up",
    )(meta, xb, wg, wu)


def _mm_plain_kernel(a_ref, w_ref, o_ref):
    o_ref[...] = _dot(a_ref[...], w_ref[...])


def _moe_down(act, meta, wd, *, tm, tn):
    rows, f = act.shape
    d = wd.shape[-1]
    tn = _row_tile(d, tn)
    n_blk = rows // tm
    in_specs, o_spec = _grouped_specs(n_blk, tm, f, tn, [wd])
    grid_spec = pltpu.PrefetchScalarGridSpec(
        num_scalar_prefetch=1, grid=(d // tn, n_blk), in_specs=in_specs, out_specs=o_spec)
    return pl.pallas_call(
        functools.partial(_grouped_kernel, body=_mm_plain_kernel, n_blk=n_blk),
        grid_spec=grid_spec,
        out_shape=jax.ShapeDtypeStruct((rows, d), F32),
        compiler_params=_params("arbitrary", "arbitrary"),
        name="moe_down",
    )(meta, act, wd)


def _router_kernel(x_ref, w_ref, o_ref):
    lg = _dot(x_ref[...], w_ref[...])
    lane = lax.broadcasted_iota(jnp.int32, lg.shape, 1)
    x = jnp.where(lane < N_EXPERTS, lg, -jnp.inf)
    v0 = jnp.max(x, axis=-1, keepdims=True)
    i0 = jnp.min(jnp.where(x == v0, lane, LANES), axis=-1, keepdims=True)
    x1 = jnp.where(lane == i0, -jnp.inf, x)
    v1 = jnp.max(x1, axis=-1, keepdims=True)
    i1 = jnp.min(jnp.where(x1 == v1, lane, LANES), axis=-1, keepdims=True)
    e1 = jnp.exp(v1 - v0)
    g0 = 1.0 / (1.0 + e1)
    g1 = e1 / (1.0 + e1)
    o_ref[...] = jnp.where(lane == 0, g0,
                           jnp.where(lane == 1, g1,
                                     jnp.where(lane == 2, i0.astype(F32),
                                               jnp.where(lane == 3, i1.astype(F32), 0.0))))


def _router(h, w_router, tm):
    n, d = h.shape
    w_pad = jnp.pad(w_router, ((0, 0), (0, LANES - N_EXPERTS)))
    out = pl.pallas_call(
        _router_kernel,
        grid=(n // tm,),
        in_specs=[pl.BlockSpec((tm, d), lambda i: (i, 0)),
                  pl.BlockSpec((d, LANES), lambda i: (0, 0))],
        out_specs=pl.BlockSpec((tm, LANES), lambda i: (i, 0)),
        out_shape=jax.ShapeDtypeStruct((n, LANES), F32),
        compiler_params=_params("arbitrary"),
        name="moe_router",
    )(h, w_pad)
    return out[:, :TOP_K], out[:, TOP_K:2 * TOP_K].astype(jnp.int32)


def _combine_norm_kernel(x_ref, y0_ref, y1_ref, w_ref, gate_ref, g_ref, o_ref):
    w = w_ref[...]
    y = w[:, 0:1] * y0_ref[...] + w[:, 1:2] * y1_ref[...]
    xo = x_ref[...] + gate_ref[...] * y
    o_ref[...] = _rms(xo) * g_ref[...]


def _combine_norm(x, y0, y1, wts, mod, kind, gate_chunk, g, *, tm, rows_per_batch):
    m, d = x.shape
    row = pl.BlockSpec((tm, d), lambda i, j: (i, 0))
    return pl.pallas_call(
        _combine_norm_kernel,
        grid=(m // tm, 1),
        in_specs=[row, row, row,
                  pl.BlockSpec((tm, TOP_K), lambda i, j: (i, 0)),
                  _mod_spec(mod, kind, gate_chunk, d, tm, rows_per_batch),
                  pl.BlockSpec((1, d), lambda i, j: (0, 0))],
        out_specs=row,
        out_shape=jax.ShapeDtypeStruct((m, d), F32),
        compiler_params=_params("arbitrary", "arbitrary"),
        name="moe_combine_final_norm",
    )(x, y0, y1, wts, mod, g.reshape(1, d))


KV_SLOTS = 2 * NSA_KV_GROUPS


def _slot_rows(ref, slot, n_tok, first_tok=0):
    return ref[pl.ds(first_tok * KV_SLOTS + slot, n_tok, stride=KV_SLOTS), :]


def _compress_native_kernel(*refs, n_in):
    w_ref, o_ref = refs[n_in], refs[n_in + 1]
    hd = NSA_HEAD_DIM
    row = 0
    for x_ref in refs[:n_in]:
        n_tok = x_ref.shape[0] // KV_SLOTS
        nb = n_tok // NSA_BLOCK
        for slot in range(KV_SLOTS):
            x = _slot_rows(x_ref, slot, n_tok)
            w = w_ref[slot // NSA_KV_GROUPS]
            pooled = jnp.sum(x.reshape(nb, NSA_BLOCK, hd) * w[None], axis=1)
            o_ref[row:row + nb, slot * hd:(slot + 1) * hd] = pooled
        row += nb


def _compress_pages_kernel(pt_ref, *refs, n_in):
    del pt_ref
    _compress_native_kernel(*refs, n_in=n_in)


def _compress_tokens(kv, w_pool, tokens_per_step):
    b, rows, hd = kv.shape
    t = rows // KV_SLOTS
    nb = tokens_per_step // NSA_BLOCK
    return pl.pallas_call(
        functools.partial(_compress_native_kernel, n_in=1),
        grid=(b, t // tokens_per_step),
        in_specs=[pl.BlockSpec((None, tokens_per_step * KV_SLOTS, hd), lambda i, s: (i, s, 0)),
                  pl.BlockSpec(w_pool.shape, lambda i, s: (0, 0, 0))],
        out_specs=pl.BlockSpec((None, nb, KV_SLOTS * hd), lambda i, s: (i, s, 0)),
        out_shape=jax.ShapeDtypeStruct((b, t // NSA_BLOCK, KV_SLOTS * hd), F32),
        compiler_params=_params("arbitrary", "arbitrary"),
        name="nsa_compress_prompt",
    )(kv, w_pool)


def _compress_pages(cache, page_table, w_pool, pages_per_step):
    bsz, n_pages = page_table.shape
    rows, hd = cache.shape[1:]
    p = pages_per_step
    bpp = PAGE_SIZE // NSA_BLOCK
    kvw = KV_SLOTS * hd

    def page_spec(k):
        return pl.BlockSpec((None, rows, hd), lambda i, s, pt: (pt[i, s * p + k], 0, 0))

    grid_spec = pltpu.PrefetchScalarGridSpec(
        num_scalar_prefetch=1,
        grid=(bsz, n_pages // p),
        in_specs=[page_spec(k) for k in range(p)] + [pl.BlockSpec(w_pool.shape, lambda i, s, pt: (0, 0, 0))],
        out_specs=pl.BlockSpec((None, p * bpp, kvw), lambda i, s, pt: (i, s, 0)))
    return pl.pallas_call(
        functools.partial(_compress_pages_kernel, n_in=p),
        grid_spec=grid_spec,
        out_shape=jax.ShapeDtypeStruct((bsz, n_pages * bpp, kvw), F32),
        compiler_params=_params("arbitrary", "arbitrary"),
        name="nsa_compress_pages",
    )(page_table, *([cache] * p), w_pool)


def _select_blocks(score, n_valid):
    rows, width = score.shape
    col = lax.broadcasted_iota(jnp.int32, (rows, width), 1)
    rank = jnp.zeros((rows, width), F32)
    for m in range(n_valid):
        c = score[:, m:m + 1]
        beats = jnp.where(c > score, 1.0, jnp.where((c == score) & (col > m), 1.0, 0.0))
        rank = rank + beats
    return jnp.where((rank < NSA_TOP_BLOCKS) & (col < n_valid), 1.0, 0.0)


def _softmax_rows(s):
    e = jnp.exp(s - jnp.max(s, axis=-1, keepdims=True))
    return e, jnp.sum(e, axis=-1, keepdims=True)


def _compressed_branch(q4, ck, cv, tpos, reps):
    scale = NSA_HEAD_DIM ** -0.5
    nc = ck.shape[0]
    tq = q4.shape[0] // reps
    s = _dot_nt(q4, ck) * scale
    blk = lax.broadcasted_iota(jnp.int32, (1, nc), 1)
    ok = ((blk + 1) * NSA_BLOCK - 1) <= tpos
    e, l = _softmax_rows(jnp.where(ok, s, NEG_INF))
    p = jnp.where(ok, e / l, 0.0)
    o_c = _dot(p, cv)
    imp = p[0:tq]
    for r in range(1, reps):
        imp = imp + p[r * tq:(r + 1) * tq]
    return o_c, imp


def _compressed_select_t(q4, ck, cv, qpos_row, reps):
    scale = NSA_HEAD_DIM ** -0.5
    nc = ck.shape[0]
    tq = q4.shape[0] // reps
    tpos = jnp.concatenate([qpos_row] * reps, axis=1)
    blk = lax.broadcasted_iota(jnp.int32, (nc, 1), 0)
    s = _dot_nt(ck, q4) * scale
    ok = ((blk + 1) * NSA_BLOCK - 1) <= tpos
    s = jnp.where(ok, s, NEG_INF)
    e = jnp.exp(s - jnp.max(s, axis=0, keepdims=True))
    p = jnp.where(ok, e / jnp.sum(e, axis=0, keepdims=True), 0.0)
    o_c = lax.dot_general(p.astype(BF16), cv.astype(BF16), _TN, preferred_element_type=F32)
    imp = p[:, 0:tq]
    for r in range(1, reps):
        imp = imp + p[:, r * tq:(r + 1) * tq]
    cur = qpos_row // NSA_BLOCK
    forced = (blk == 0) | (blk == cur) | (blk == cur - 1)
    score = jnp.where(forced, FORCE_SCORE, jnp.where(blk <= cur, imp, -1.0))
    row = lax.broadcasted_iota(jnp.int32, (nc, tq), 0)
    rank = jnp.zeros((nc, tq), F32)
    for m in range(nc):
        c = score[m:m + 1, :]
        rank = rank + jnp.where(c > score, 1.0, jnp.where((c == score) & (row > m), 1.0, 0.0))
    return o_c, jnp.where(rank < NSA_TOP_BLOCKS, 1.0, 0.0)


def _block_scores(imp, qpos, n_blocks_pad):
    tq, n_imp = imp.shape
    if n_blocks_pad > n_imp:
        imp = jnp.concatenate([imp, jnp.zeros((tq, n_blocks_pad - n_imp), F32)], axis=1)
    blk = lax.broadcasted_iota(jnp.int32, (1, n_blocks_pad), 1)
    cur = qpos // NSA_BLOCK
    forced = (blk == 0) | (blk == cur) | (blk == cur - 1)
    return jnp.where(forced, FORCE_SCORE, jnp.where(blk <= cur, imp, -1.0))


def _nsa_prompt_kernel(q_ref, ck_ref, cv_ref, ks_ref, vs_ref, kw_ref, vw_ref, gate_ref, o_ref, *, tq, kc, seq):
    hd, reps = NSA_HEAD_DIM, NSA_REP
    c2 = (hd ** -0.5) * LOG2_E
    q0 = pl.program_id(2) * tq
    qf = q_ref[...]
    q4 = jnp.concatenate([qf[:, r * hd:(r + 1) * hd] for r in range(reps)], axis=0).astype(BF16)
    qpos = q0 + lax.broadcasted_iota(jnp.int32, (tq, 1), 0)
    m_rows = reps * tq
    stack = lambda x: jnp.concatenate([x] * reps, axis=0)

    n_blocks = seq // NSA_BLOCK
    qpos_row = q0 + lax.broadcasted_iota(jnp.int32, (1, tq), 1)
    o_c, sel = _compressed_select_t(q4, ck_ref[...], cv_ref[...], qpos_row, reps)
    sel = sel.astype(BF16)

    blk_row = lax.broadcasted_iota(jnp.int32, (n_blocks, 1), 0)

    def chunk(c, carry):
        m_i, l_i, acc = carry
        k0 = pl.multiple_of(c * kc, kc)
        kk = ks_ref[pl.ds(k0, kc), :]
        vv = vs_ref[pl.ds(k0, kc), :]
        kpos = k0 + lax.broadcasted_iota(jnp.int32, (1, kc), 1)
        expand = jnp.where(kpos // NSA_BLOCK == blk_row, 1.0, 0.0).astype(BF16)
        selk = lax.dot_general(sel, expand, _TN, preferred_element_type=F32)
        bias = jnp.where(kpos <= qpos, (1.0 - selk) * NEG_INF, NEG_INF)
        s = _dot_nt(q4, kk) * c2 + stack(bias)
        m_new = jnp.maximum(m_i, jnp.max(s, axis=-1, keepdims=True))
        alpha = jnp.exp2(m_i - m_new)
        p = jnp.exp2(s - m_new)
        l_new = alpha * l_i + jnp.sum(p, axis=-1, keepdims=True)
        return m_new, l_new, alpha * acc + _dot(p, vv)

    n_chunks = (q0 + tq + kc - 1) // kc
    init = (jnp.full((m_rows, 1), NEG_INF, F32), jnp.zeros((m_rows, 1), F32), jnp.zeros((m_rows, hd), F32))
    _, l_s, acc_s = lax.fori_loop(0, n_chunks, chunk, init)
    o_s = acc_s / l_s

    span = NSA_WINDOW + tq
    w0 = pl.multiple_of(jnp.maximum(q0 - NSA_WINDOW, 0), tq)
    dist = qpos - (w0 + lax.broadcasted_iota(jnp.int32, (1, span), 1))
    bias = jnp.where(dist >= 0, jnp.where(dist < NSA_WINDOW, 0.0, NEG_INF), NEG_INF)
    s = _dot_nt(q4, kw_ref[pl.ds(w0, span), :]) * c2 + stack(bias)
    e = jnp.exp2(s - jnp.max(s, axis=-1, keepdims=True))
    o_w = _dot(e, vw_ref[pl.ds(w0, span), :]) / jnp.sum(e, axis=-1, keepdims=True)

    gate = gate_ref[...]
    for r in range(reps):
        rows = slice(r * tq, (r + 1) * tq)
        o = (gate[:, 3 * r:3 * r + 1] * o_c[rows] + gate[:, 3 * r + 1:3 * r + 2] * o_s[rows]
             + gate[:, 3 * r + 2:3 * r + 3] * o_w[rows])
        o_ref[:, r * hd:(r + 1) * hd] = o.astype(o_ref.dtype)


def _nsa_prompt_attention(q, ckv, kv_s, kv_w, gates, *, tq, kc):
    b, t, _ = q.shape
    g_, hd = NSA_KV_GROUPS, NSA_HEAD_DIM
    qw = NSA_REP * hd
    nc = ckv.shape[1]
    keys = pl.BlockSpec((None, t, hd), lambda i, g, s: (i, 0, g))
    vals = pl.BlockSpec((None, t, hd), lambda i, g, s: (i, 0, g_ + g))
    return pl.pallas_call(
        functools.partial(_nsa_prompt_kernel, tq=tq, kc=kc, seq=t),
        grid=(b, g_, t // tq),
        in_specs=[pl.BlockSpec((None, tq, qw), lambda i, g, s: (i, s, g)),
                  pl.BlockSpec((None, nc, hd), lambda i, g, s: (i, 0, g)),
                  pl.BlockSpec((None, nc, hd), lambda i, g, s: (i, 0, g_ + g)),
                  keys, vals, keys, vals,
                  pl.BlockSpec((None, tq, LANES), lambda i, g, s: (i, s, g))],
        out_specs=pl.BlockSpec((None, tq, qw), lambda i, g, s: (i, s, g)),
        out_shape=jax.ShapeDtypeStruct((b, t, NSA_Q_WIDTH), BF16),
        compiler_params=_params("arbitrary", "arbitrary", "arbitrary"),
        name="nsa_prompt_attention",
    )(q, ckv, ckv, kv_s, kv_s, kv_w, kv_w, gates)


def _nsa_sample_select_kernel(q_ref, ckv_ref, oc_ref, bias_ref, *, t_new, past, n_blocks, n_pad, chunk):
    hd, reps = NSA_HEAD_DIM, NSA_REP
    qpos = past + lax.broadcasted_iota(jnp.int32, (t_new, 1), 0)
    tpos = jnp.concatenate([qpos] * reps, axis=0)
    sels = []
    for g in range(NSA_KV_GROUPS):
        ck = ckv_ref[:, g * hd:(g + 1) * hd]
        cv = ckv_ref[:, NSA_KV_WIDTH + g * hd:NSA_KV_WIDTH + (g + 1) * hd]
        o_c, imp = _compressed_branch(q_ref[g].astype(BF16), ck, cv, tpos, reps)
        oc_ref[g] = o_c
        sels.append(_select_blocks(_block_scores(imp, qpos, n_pad), n_blocks).astype(BF16))
    n_keys = bias_ref.shape[-1]
    blk_row = lax.broadcasted_iota(jnp.int32, (n_pad, 1), 0)
    for k0 in range(0, n_keys, chunk):
        width = min(chunk, n_keys - k0)
        kpos = k0 + lax.broadcasted_iota(jnp.int32, (1, width), 1)
        expand = jnp.where(kpos // NSA_BLOCK == blk_row, 1.0, 0.0).astype(BF16)
        for g in range(NSA_KV_GROUPS):
            selk = jnp.dot(sels[g], expand, preferred_element_type=F32)
            bias_ref[g, :, k0:k0 + width] = jnp.where(kpos <= qpos, (1.0 - selk) * NEG_INF, NEG_INF)


def _nsa_sample_select(q4, ckv, *, t_new, past, n_blocks, n_pad, n_keys, chunk):
    b = q4.shape[0]
    g_, hd = NSA_KV_GROUPS, NSA_HEAD_DIM
    rows = NSA_REP * t_new
    nc = ckv.shape[1]
    return pl.pallas_call(
        functools.partial(_nsa_sample_select_kernel, t_new=t_new, past=past, n_blocks=n_blocks, n_pad=n_pad,
                          chunk=chunk),
        grid=(b,),
        in_specs=[pl.BlockSpec((None, g_, rows, hd), lambda i: (i, 0, 0, 0)),
                  pl.BlockSpec((None, nc, 2 * NSA_KV_WIDTH), lambda i: (i, 0, 0))],
        out_specs=[pl.BlockSpec((None, g_, rows, hd), lambda i: (i, 0, 0, 0)),
                   pl.BlockSpec((None, g_, t_new, n_keys), lambda i: (i, 0, 0, 0))],
        out_shape=[jax.ShapeDtypeStruct((b, g_, rows, hd), F32),
                   jax.ShapeDtypeStruct((b, g_, t_new, n_keys), F32)],
        compiler_params=_params("arbitrary"),
        name="nsa_sample_select",
    )(q4, ckv)


def _nsa_sample_attend_kernel(pt_ref, *refs, n_pg, t_new, past):
    del pt_ref
    pages = refs[:n_pg]
    (q_ref, bias_ref, bias_new_ref, oc_ref, new_s_ref, win_ref, new_w_ref, gate_ref,
     o_ref, win_out_ref, m_ref, l_ref, acc_ref) = refs[n_pg:]
    hd, reps, g_ = NSA_HEAD_DIM, NSA_REP, NSA_KV_GROUPS
    kvw = NSA_KV_WIDTH
    scale = hd ** -0.5
    c2 = scale * LOG2_E
    step = pl.program_id(1)
    qpos = past + lax.broadcasted_iota(jnp.int32, (t_new, 1), 0)
    tpos = jnp.concatenate([qpos] * reps, axis=0)
    stack = lambda x: jnp.concatenate([x] * reps, axis=0)

    def online(keys, vals, biases):
        groups = range(g_)
        m_i = [m_ref[g] for g in groups]
        l_i = [l_ref[g] for g in groups]
        acc_i = [acc_ref[g] for g in groups]
        s = [_dot_nt(q_ref[g], keys[g]) * c2 + stack(biases[g]) for g in groups]
        m_new = [jnp.maximum(m_i[g], jnp.max(s[g], axis=-1, keepdims=True)) for g in groups]
        alpha = [jnp.exp2(m_i[g] - m_new[g]) for g in groups]
        p = [jnp.exp2(s[g] - m_new[g]) for g in groups]
        pv = [_dot(p[g], vals[g]) for g in groups]
        for g in groups:
            l_ref[g] = alpha[g] * l_i[g] + jnp.sum(p[g], axis=-1, keepdims=True)
            acc_ref[g] = alpha[g] * acc_i[g] + pv[g]
            m_ref[g] = m_new[g]

    def new_rows(ref, c0):
        x = ref[:, c0:c0 + hd]
        return jnp.concatenate([x, jnp.zeros((PAGE_SIZE - t_new, hd), x.dtype)], axis=0)

    @pl.when(step == 0)
    def _():
        m_ref[...] = jnp.full(m_ref.shape, NEG_INF, F32)
        l_ref[...] = jnp.zeros(l_ref.shape, F32)
        acc_ref[...] = jnp.zeros(acc_ref.shape, F32)
        online([new_rows(new_s_ref, g * hd) for g in range(g_)],
               [new_rows(new_s_ref, kvw + g * hd) for g in range(g_)],
               [bias_new_ref[g] for g in range(g_)])

    page_rows = lambda slot: jnp.concatenate([_slot_rows(pg, slot, PAGE_SIZE).astype(BF16) for pg in pages], axis=0)
    online([page_rows(g) for g in range(g_)], [page_rows(g_ + g) for g in range(g_)],
           [bias_ref[g] for g in range(g_)])

    @pl.when(step == pl.num_programs(1) - 1)
    def _():
        wb = win_ref.shape[0] // KV_SLOTS
        wpos = past - wb + lax.broadcasted_iota(jnp.int32, (1, wb + PAGE_SIZE), 1)
        dist = tpos - wpos
        gate = gate_ref[...]
        for g in range(g_):
            kk = jnp.concatenate([_slot_rows(win_ref, g, wb), new_rows(new_w_ref, g * hd)], axis=0)
            vv = jnp.concatenate([_slot_rows(win_ref, g_ + g, wb), new_rows(new_w_ref, kvw + g * hd)], axis=0)
            s = _dot_nt(q_ref[g], kk) * scale
            ok = (wpos >= 0) & (dist >= 0) & (dist < NSA_WINDOW)
            e, l = _softmax_rows(jnp.where(ok, s, NEG_INF))
            o_w = _dot(e, vv) / l
            o_s = acc_ref[g] / l_ref[g]
            o_c = oc_ref[g]
            for r in range(reps):
                rs = slice(r * t_new, (r + 1) * t_new)
                c0 = g * LANES + 3 * r
                o = (gate[:, c0:c0 + 1] * o_c[rs] + gate[:, c0 + 1:c0 + 2] * o_s[rs]
                     + gate[:, c0 + 2:c0 + 3] * o_w[rs])
                col = (g * reps + r) * hd
                o_ref[:, col:col + hd] = o.astype(o_ref.dtype)
        keep = (wb - t_new) * KV_SLOTS
        win_out_ref[0:keep, :] = win_ref[t_new * KV_SLOTS:wb * KV_SLOTS, :]
        for slot in range(KV_SLOTS):
            win_out_ref[pl.ds(keep + slot, t_new, stride=KV_SLOTS), :] = new_w_ref[:, slot * hd:(slot + 1) * hd]


def _nsa_sample_attend(q4, bias, o_c, kv_s, win, kv_w, gates, cache, page_table, *, n_pg, past):
    b, g_, rows, hd = q4.shape
    t_new = rows // NSA_REP
    width = n_pg * PAGE_SIZE
    kvw2 = 2 * NSA_KV_WIDTH
    n_pages = page_table.shape[1]
    win_rows = win.shape[1]
    page_rows = cache.shape[1]

    def page_spec(k):
        return pl.BlockSpec((None, page_rows, hd), lambda i, s, pt: (pt[i, s * n_pg + k], 0, 0))

    per_b = lambda shape: pl.BlockSpec((None,) + shape, lambda i, s, pt: (i,) + (0,) * len(shape))
    grid_spec = pltpu.PrefetchScalarGridSpec(
        num_scalar_prefetch=1,
        grid=(b, n_pages // n_pg),
        in_specs=[page_spec(k) for k in range(n_pg)] + [
            per_b((g_, rows, hd)),
            pl.BlockSpec((None, g_, t_new, width), lambda i, s, pt: (i, 0, 0, s)),
            pl.BlockSpec((None, g_, t_new, PAGE_SIZE), lambda i, s, pt: (i, 0, 0, past // PAGE_SIZE)),
            per_b((g_, rows, hd)),
            per_b((t_new, kvw2)), per_b((win_rows, hd)), per_b((t_new, kvw2)),
            per_b((t_new, g_ * LANES))],
        out_specs=[per_b((t_new, NSA_Q_WIDTH)), per_b((win_rows, hd))],
        scratch_shapes=[pltpu.VMEM((g_, rows, 1), F32), pltpu.VMEM((g_, rows, 1), F32),
                        pltpu.VMEM((g_, rows, hd), F32)])
    return pl.pallas_call(
        functools.partial(_nsa_sample_attend_kernel, n_pg=n_pg, t_new=t_new, past=past),
        grid_spec=grid_spec,
        out_shape=[jax.ShapeDtypeStruct((b, t_new, NSA_Q_WIDTH), BF16),
                   jax.ShapeDtypeStruct((b, win_rows, hd), F32)],
        compiler_params=_params("arbitrary", "arbitrary"),
        name="nsa_sample_attention",
    )(page_table, *([cache] * n_pg), q4, bias, bias, o_c, kv_s, win, kv_w, gates)


def _log_sigmoid(x):
    return jnp.minimum(x, 0.0) - jnp.log1p(jnp.exp(-jnp.abs(x)))


def _mlstm_kernel(q_ref, k_ref, v_ref, o_ref, gc_ref, gr_ref, bc_ref, br_ref, ng_ref,
                  c0_ref, n0_ref, m0_ref, y_ref, c_out, n_out, m_out, c_s, n_s, m_s):
    nh, dk, dv = MLSTM_HEADS, MLSTM_QK_DIM, MLSTM_V_DIM
    step = pl.program_id(1)
    ln = q_ref.shape[0]

    @pl.when(step == 0)
    def _():
        c_s[...] = c0_ref[...]
        n_s[...] = n0_ref[...]
        m_s[...] = m0_ref[...]

    hi = lax.Precision.HIGHEST
    r_i = lax.broadcasted_iota(jnp.int32, (ln, ln), 0)
    c_i = lax.broadcasted_iota(jnp.int32, (ln, ln), 1)
    causal = c_i <= r_i
    lower = jnp.where(causal, 1.0, 0.0)
    upper = jnp.where(r_i <= c_i, 1.0, 0.0)
    gcol = gc_ref[...] + bc_ref[...]
    grow = gr_ref[...] + br_ref[...]
    ig_c = gcol[:, :nh]
    b_c = jnp.dot(lower, _log_sigmoid(gcol[:, nh:]), precision=hi, preferred_element_type=F32)
    ig_r = grow[:nh, :]
    b_r = jnp.dot(_log_sigmoid(grow[nh:, :]), upper, precision=hi, preferred_element_type=F32)

    heads = range(nh)
    q = [q_ref[:, h * dk:(h + 1) * dk] for h in heads]
    k = [k_ref[:, h * dk:(h + 1) * dk] * (dk ** -0.5) for h in heads]
    vb = [v_ref[:, h * dv:(h + 1) * dv].astype(BF16) for h in heads]
    qb = [x.astype(BF16) for x in q]
    c_prev = [c_s[h] for h in heads]
    n_prev = [n_s[h] for h in heads]
    m_prev = [m_s[h] for h in heads]
    bt = [b_c[:, h:h + 1] for h in heads]
    b_last = [b_c[ln - 1:ln, h:h + 1] for h in heads]

    d = [jnp.where(causal, bt[h] - b_r[h:h + 1, :] + ig_r[h:h + 1, :], -jnp.inf) for h in heads]
    inter = [bt[h] + m_prev[h] for h in heads]
    m_t = [jnp.maximum(inter[h], jnp.max(d[h], axis=-1, keepdims=True)) for h in heads]
    a = [jnp.exp(inter[h] - m_t[h]) for h in heads]
    m_new = [m_t[h][ln - 1:ln, :] for h in heads]
    ws = [jnp.exp(b_last[h] - bt[h] + ig_c[:, h:h + 1] - m_new[h]) for h in heads]
    decay = [jnp.exp(b_last[h] + m_prev[h] - m_new[h]) for h in heads]
    kw = [k[h] * ws[h] for h in heads]

    qk = [_dot_nt(qb[h], k[h]) for h in heads]
    qc = [_dot(qb[h], c_prev[h]) for h in heads]
    kwv = [lax.dot_general(kw[h].astype(BF16), vb[h], _TN, preferred_element_type=F32) for h in heads]
    w = [jnp.exp(d[h] - m_t[h]) * qk[h] for h in heads]
    wv = [_dot(w[h], vb[h]) for h in heads]

    ys = []
    for h in heads:
        num = a[h] * qc[h] + wv[h]
        den = a[h] * jnp.sum(q[h] * n_prev[h], axis=-1, keepdims=True) + jnp.sum(w[h], axis=-1, keepdims=True)
        hh = num / jnp.maximum(jnp.abs(den), jnp.exp(-m_t[h]))
        hn = _rms(hh) * ng_ref[:, h * dv:(h + 1) * dv]
        ys.append((hn * _sigmoid(o_ref[:, h * dv:(h + 1) * dv])).astype(y_ref.dtype))

    for h in heads:
        y_ref[:, h * dv:(h + 1) * dv] = ys[h]
        c_s[h] = decay[h] * c_prev[h] + kwv[h]
        n_s[h] = decay[h] * n_prev[h] + jnp.sum(kw[h], axis=0, keepdims=True)
        m_s[h] = m_new[h]

    @pl.when(step == pl.num_programs(1) - 1)
    def _():
        c_out[...] = c_s[...]
        n_out[...] = n_s[...]
        m_out[...] = m_s[...]


def _mlstm(proj, graw, b_gate, norm_g, c0, n0, m0, *, chunk):
    b, t, _ = proj.shape
    nh, dk, dv = MLSTM_HEADS, MLSTM_QK_DIM, MLSTM_V_DIM
    nchunk = t // chunk
    qw, vw = nh * dk, nh * dv
    grow = jnp.transpose(graw.reshape(b, nchunk, chunk, 2 * nh), (0, 1, 3, 2))
    state = lambda shape: pl.BlockSpec((None,) + shape, lambda i, s: (i,) + (0,) * len(shape))
    const = lambda shape: pl.BlockSpec(shape, lambda i, s: (0,) * len(shape))
    return pl.pallas_call(
        _mlstm_kernel,
        grid=(b, nchunk),
        in_specs=[pl.BlockSpec((None, chunk, qw), lambda i, s: (i, s, 0)),
                  pl.BlockSpec((None, chunk, qw), lambda i, s: (i, s, 1)),
                  pl.BlockSpec((None, chunk, vw), lambda i, s: (i, s, 2 * qw // vw)),
                  pl.BlockSpec((None, chunk, vw), lambda i, s: (i, s, 2 * qw // vw + 1)),
                  pl.BlockSpec((None, chunk, 2 * nh), lambda i, s: (i, s, 0)),
                  pl.BlockSpec((None, None, 2 * nh, chunk), lambda i, s: (i, s, 0, 0)),
                  const((1, 2 * nh)), const((2 * nh, 1)), const((1, vw)),
                  state((nh, dk, dv)), state((nh, 1, dk)), state((nh, 1, 1))],
        out_specs=[pl.BlockSpec((None, chunk, vw), lambda i, s: (i, s, 0)),
                   state((nh, dk, dv)), state((nh, 1, dk)), state((nh, 1, 1))],
        out_shape=[jax.ShapeDtypeStruct((b, t, vw), BF16),
                   jax.ShapeDtypeStruct((b, nh, dk, dv), F32),
                   jax.ShapeDtypeStruct((b, nh, 1, dk), F32),
                   jax.ShapeDtypeStruct((b, nh, 1, 1), F32)],
        scratch_shapes=[pltpu.VMEM((nh, dk, dv), F32), pltpu.VMEM((nh, 1, dk), F32),
                        pltpu.VMEM((nh, 1, 1), F32)],
        compiler_params=_params("arbitrary", "arbitrary"),
        name="mlstm",
    )(proj, proj, proj, proj, graw, grow, b_gate.reshape(1, 2 * nh), b_gate.reshape(2 * nh, 1),
      norm_g.reshape(1, vw), c0, n0.reshape(b, nh, 1, dk), m0.reshape(b, nh, 1, 1))


def _rope_tables(pos):
    half = NSA_HEAD_DIM // 2
    freq = ROPE_THETA ** (-jnp.arange(half, dtype=F32) / half)
    ang = pos.astype(F32)[:, None] * freq[None, :]
    cos, sin = jnp.cos(ang), jnp.sin(ang)
    return jnp.concatenate([cos, cos], axis=-1), jnp.concatenate([-sin, sin], axis=-1)


def _gate_weights(w_in):
    d = w_in.shape[0]
    per_group = NSA_REP * 3
    w = w_in[:, NSA_Q_WIDTH + 6 * NSA_KV_WIDTH:].reshape(d, NSA_KV_GROUPS, per_group)
    w = jnp.pad(w, ((0, 0), (0, 0), (0, LANES - per_group)))
    return w.reshape(d, NSA_KV_GROUPS * LANES)


def _row_tile(m, pref):
    t = min(pref, m)
    while m % t:
        t //= 2
    return t


class _Group:
    def __init__(self, x, mod_rows, kind):
        self.b, self.t, self.d = x.shape
        self.x = x.reshape(self.b * self.t, self.d)
        self.kind = kind
        self.mod_rows = mod_rows

    def mod(self, layer):
        m = self.mod_rows[layer]
        if self.kind == "batch":
            return m.reshape(self.b, 1, m.shape[-1])
        return jnp.repeat(m, self.t, axis=0)


def kernel(x_prompt, x_sample, c_prompt, c_sample, cache_nsa_cmp_kv, cache_nsa_sel_kv, page_table, state_nsa_win_kv, state_mlstm_c, state_mlstm_n, state_mlstm_m, ada_w, ada_b, norm_mix_g, norm_ffn_g, norm_final_g, nsa_w_in, nsa_w_pool, nsa_w_out, mlstm_w_in, mlstm_b_gate, mlstm_norm_g, mlstm_w_out, ffn_w_gate, ffn_w_up, ffn_w_down, moe_w_router, moe_w_gate, moe_w_up, moe_w_down):
    bp, tp, d = x_prompt.shape
    bs, ts, _ = x_sample.shape
    past = page_table.shape[1] * PAGE_SIZE
    assert tp % NSA_BLOCK == 0 and tp >= NSA_WINDOW + 128 and ts < NSA_BLOCK and ts % SUBLANES == 0
    assert past % NSA_BLOCK == 0 and state_nsa_win_kv.shape[2] == NSA_WINDOW
    assert ada_w.shape[0] == 2 and nsa_w_in.shape[0] == 1 and mlstm_w_in.shape[0] == 1

    mod_all = _adaln(jnp.concatenate([c_prompt, c_sample], axis=0), ada_w, ada_b)
    groups = [_Group(x_prompt, mod_all[:, :bp], "batch"), _Group(x_sample, mod_all[:, bp:], "row")]
    tms = [_row_tile(tp, 1024), bs * ts]
    tes = [_row_tile(tp, 512), bs * ts]

    w_gates = _gate_weights(nsa_w_in[0])
    g_, hd = NSA_KV_GROUPS, NSA_HEAD_DIM
    kvw2 = 2 * NSA_KV_WIDTH
    outs = {}
    for gi, (grp, tm, te) in enumerate(zip(groups, tms, tes)):
        b, t = grp.b, grp.t
        mod = grp.mod(0)
        h = _modulate(grp.x, norm_mix_g[0], mod, grp.kind, 0, 1, te, t)
        pos = jnp.arange(t) if gi == 0 else past + jnp.arange(t)
        cos, sin = _rope_tables(pos)
        if gi == 1:
            cos, sin = jnp.tile(cos, (b, 1)), jnp.tile(sin, (b, 1))
        q = _mm_rope(h, nsa_w_in, cos, sin, col0=0, n_out=NSA_Q_WIDTH, rope_cols=NSA_Q_WIDTH,
                     tm=tm, tn=512, out_dtype=BF16, name="nsa_q_proj").reshape(b, t, NSA_Q_WIDTH)
        kvp = lambda br, row_major: _kv_proj(h, nsa_w_in, cos, sin, col0=NSA_Q_WIDTH + br * kvw2, tm=te,
                                             row_major=row_major, name="nsa_kv_proj")
        cache_rows = lambda a: a.reshape(b, t * KV_SLOTS, hd)
        (kv_c_n,) = kvp(0, False)
        kv_s_n, kv_s = kvp(1, True)
        kv_w_n, kv_w = kvp(2, True)
        kv_c_n, kv_s_n, kv_w_n = cache_rows(kv_c_n), cache_rows(kv_s_n), cache_rows(kv_w_n)
        kv_s, kv_w = kv_s.reshape(b, t, kvw2), kv_w.reshape(b, t, kvw2)
        gates = _mm(h, w_gates, tm=tm, tn=g_ * LANES, act="sigmoid", name="nsa_gates").reshape(b, t, g_ * LANES)
        if gi == 0:
            ckv = _compress_tokens(kv_c_n, nsa_w_pool[0], 8 * NSA_BLOCK)
            att = _nsa_prompt_attention(q, ckv, kv_s, kv_w, gates, tq=128, kc=512)
            win_out = kv_w_n[:, (t - min(NSA_WINDOW, t)) * KV_SLOTS:]
        else:
            n_phys = cache_nsa_cmp_kv.shape[1]
            native = lambda a, lead: a.reshape(lead, -1, hd)
            ckv = _compress_pages(native(cache_nsa_cmp_kv, n_phys), page_table, nsa_w_pool[0],
                                  _row_tile(page_table.shape[1], 16))
            n_blocks = past // NSA_BLOCK + 1
            n_pad = -(-n_blocks // LANES) * LANES
            q4 = jnp.transpose(q.reshape(b, t, g_, NSA_REP, hd), (0, 2, 3, 1, 4)).reshape(b, g_, NSA_REP * t, hd)
            n_pg = 8
            o_c, bias = _nsa_sample_select(q4, ckv, t_new=t, past=past, n_blocks=n_blocks, n_pad=n_pad,
                                           n_keys=past + PAGE_SIZE, chunk=n_pg * PAGE_SIZE)
            att, win_out = _nsa_sample_attend(
                q4, bias, o_c, kv_s, native(state_nsa_win_kv, b), kv_w, gates,
                native(cache_nsa_sel_kv, n_phys), page_table, n_pg=n_pg, past=past)
        as_cache = lambda a: a.reshape(1, b, -1, 2, g_, hd)
        outs[gi] = (as_cache(kv_c_n), as_cache(kv_s_n), as_cache(win_out))
        x1 = _mm_res(att.reshape(b * t, NSA_Q_WIDTH), nsa_w_out, grp.x, mod, grp.kind, 2,
                     tm=tm, tn=512, rows_per_batch=t)
        h2 = _modulate(x1, norm_ffn_g[0], mod, grp.kind, 3, 4, te, t)
        act = _swiglu_up(h2, ffn_w_gate, ffn_w_up, tm=tm, tn=512)
        grp.x = _mm_res(act, ffn_w_down, x1, mod, grp.kind, 5, tm=te, tn=512, rows_per_batch=t)

    nh, dk, dv = MLSTM_HEADS, MLSTM_QK_DIM, MLSTM_V_DIM
    n_main = 2 * nh * dk + 2 * nh * dv
    w_g = jnp.pad(mlstm_w_in[0][:, n_main:], ((0, 0), (0, LANES - 2 * nh)))
    states = {}
    hs2, mods = [], []
    for gi, (grp, tm, te) in enumerate(zip(groups, tms, tes)):
        b, t = grp.b, grp.t
        mod = grp.mod(1)
        h = _modulate(grp.x, norm_mix_g[1], mod, grp.kind, 0, 1, te, t)
        proj = _mm(h, mlstm_w_in, n_out=n_main, tm=tm, tn=512, name="mlstm_in_proj").reshape(b, t, n_main)
        graw = _mm(h, w_g, tm=tm, tn=LANES, name="mlstm_gates")[:, :2 * nh].reshape(b, t, 2 * nh)
        if gi == 0:
            c0 = jnp.zeros((b, nh, dk, dv), F32)
            n0 = jnp.zeros((b, nh, dk), F32)
            m0 = jnp.zeros((b, nh), F32)
            chunk = MLSTM_CHUNK
        else:
            c0, n0, m0 = state_mlstm_c[0], state_mlstm_n[0], state_mlstm_m[0]
            chunk = t
        y, c_new, n_new, m_new = _mlstm(proj, graw, mlstm_b_gate[0], mlstm_norm_g[0], c0, n0, m0, chunk=chunk)
        states[gi] = (c_new[None], n_new.reshape(1, b, nh, dk), m_new.reshape(1, b, nh))
        grp.x = _mm_res(y.reshape(b * t, nh * dv), mlstm_w_out, grp.x, mod, grp.kind, 2,
                        tm=tm, tn=512, rows_per_batch=t)
        hs2.append(_modulate(grp.x, norm_ffn_g[1], mod, grp.kind, 3, 4, te, t))
        mods.append(mod)

    h_all = jnp.concatenate(hs2, axis=0)
    n_tok = h_all.shape[0]
    tm_moe = MOE_ROW_BLOCK
    wts, top_e = _router(h_all, moe_w_router[0], _row_tile(n_tok, 256))
    n_asg = n_tok * TOP_K
    flat_e = top_e.reshape(n_asg)
    onehot = (flat_e[:, None] == jnp.arange(N_EXPERTS)[None, :]).astype(jnp.int32)
    within = jnp.take_along_axis(jnp.cumsum(onehot, axis=0), flat_e[:, None], axis=1)[:, 0] - 1
    counts = jnp.sum(onehot, axis=0)
    padded = (counts + tm_moe - 1) // tm_moe * tm_moe
    p_end = jnp.cumsum(padded)
    dest = (p_end - padded)[flat_e] + within
    n_blk = -(-(n_asg + N_EXPERTS * (tm_moe - 1)) // tm_moe)
    n_rows = n_blk * tm_moe
    row_tok = jnp.full((n_rows,), n_tok, jnp.int32).at[dest].set(jnp.arange(n_asg, dtype=jnp.int32) // TOP_K)
    blk_start = jnp.arange(n_blk, dtype=jnp.int32) * tm_moe
    blk_e = jnp.minimum(jnp.sum((p_end[None, :] <= blk_start[:, None]).astype(jnp.int32), axis=1), N_EXPERTS - 1)
    meta = jnp.concatenate([blk_e, p_end[-1:] // tm_moe]).astype(jnp.int32)
    xb = jnp.concatenate([h_all, jnp.zeros((1, d), h_all.dtype)], axis=0)[row_tok]
    act = _moe_up(xb, meta, moe_w_gate, moe_w_up, tm=tm_moe, tn=1024)
    yb = _moe_down(act, meta, moe_w_down, tm=tm_moe, tn=512)
    dest2 = dest.reshape(n_tok, TOP_K)

    finals = []
    off = 0
    for gi, (grp, te) in enumerate(zip(groups, tes)):
        m = grp.b * grp.t
        dg = dest2[off:off + m]
        fin = _combine_norm(grp.x, yb[dg[:, 0]], yb[dg[:, 1]], wts[off:off + m], mods[gi], grp.kind, 5,
                            norm_final_g, tm=te, rows_per_batch=grp.t)
        finals.append(fin.reshape(grp.b, grp.t, d))
        off += m

    return (finals[0], finals[1],
            outs[0][0], outs[0][1], outs[0][2], states[0][0], states[0][1], states[0][2],
            outs[1][0], outs[1][1], outs[1][2], states[1][0], states[1][1], states[1][2])
```

```python
import functools

import jax
import jax.numpy as jnp
from jax import lax
from jax.experimental import pallas as pl
from jax.experimental.pallas import tpu as pltpu

F32 = jnp.float32
BF16 = jnp.bfloat16

PAGE_SIZE = 128
NSA_HEADS = 16
NSA_HEAD_DIM = 128
NSA_KV_GROUPS = 4
NSA_REP = NSA_HEADS // NSA_KV_GROUPS
NSA_BLOCK = 64
NSA_TOP_BLOCKS = 16
NSA_WINDOW = 512
NSA_Q_WIDTH = NSA_HEADS * NSA_HEAD_DIM
NSA_KV_WIDTH = NSA_KV_GROUPS * NSA_HEAD_DIM
ROPE_THETA = 10000.0
FORCE_SCORE = 1.0e4
NEG_INF = -1.0e30
MLSTM_HEADS = 8
MLSTM_QK_DIM = 128
MLSTM_V_DIM = 256
MLSTM_CHUNK = 64
N_EXPERTS = 8
TOP_K = 2
MOE_ROW_BLOCK = 512
NORM_EPS = 1e-6
LOG2_E = 1.4426950408889634

LANES = 128
SUBLANES = 8
VMEM_LIMIT_BYTES = 52 * 1024 * 1024

_NT = (((1,), (1,)), ((), ()))
_TN = (((0,), (0,)), ((), ()))


def _params(*sem):
    return pltpu.CompilerParams(dimension_semantics=sem, vmem_limit_bytes=VMEM_LIMIT_BYTES)


def _dot(a, b):
    return jnp.dot(a.astype(BF16), b.astype(BF16), preferred_element_type=F32)


def _dot_nt(a, b):
    return lax.dot_general(a.astype(BF16), b.astype(BF16), _NT, preferred_element_type=F32)


def _sigmoid(x):
    return 1.0 / (1.0 + jnp.exp(-x))


def _silu(x):
    return x * _sigmoid(x)


def _adaln_kernel(c_ref, w_ref, b_ref, o_ref):
    o_ref[...] = _dot(_silu(c_ref[...]), w_ref[...]) + b_ref[...]


def _adaln(c_all, ada_w, ada_b):
    depth, d, n = ada_w.shape
    tn = _row_tile(n, 1024)
    rows = c_all.shape[0]
    return pl.pallas_call(
        _adaln_kernel,
        grid=(depth, n // tn),
        in_specs=[pl.BlockSpec((rows, d), lambda l, j: (0, 0)),
                  pl.BlockSpec((None, d, tn), lambda l, j: (l, 0, j)),
                  pl.BlockSpec((None, 1, tn), lambda l, j: (l, 0, j))],
        out_specs=pl.BlockSpec((None, rows, tn), lambda l, j: (l, 0, j)),
        out_shape=jax.ShapeDtypeStruct((depth, rows, n), F32),
        compiler_params=_params("arbitrary", "arbitrary"),
        name="adaln",
    )(c_all, ada_w, ada_b.reshape(depth, 1, n))


def _rms(x):
    return x * lax.rsqrt(jnp.mean(x * x, axis=-1, keepdims=True) + NORM_EPS)


def _modulate_kernel(x_ref, g_ref, shift_ref, scale_ref, o_ref):
    y = _rms(x_ref[...]) * g_ref[...]
    o_ref[...] = (y * (1.0 + scale_ref[...]) + shift_ref[...]).astype(o_ref.dtype)


def _mod_spec(mod, kind, chunk, width, tm, rows_per_batch):
    d6 = mod.shape[-1]
    per_chunk = (d6 // 6) // width
    if kind == "batch":
        return pl.BlockSpec((None, 1, width),
                            lambda i, j: ((i * tm) // rows_per_batch, 0, chunk * per_chunk + j))
    return pl.BlockSpec((tm, width), lambda i, j: (i, chunk * per_chunk + j))


def _modulate(x, g, mod, kind, shift_chunk, scale_chunk, tm, rows_per_batch):
    m, d = x.shape
    return pl.pallas_call(
        _modulate_kernel,
        grid=(m // tm, 1),
        in_specs=[pl.BlockSpec((tm, d), lambda i, j: (i, 0)),
                  pl.BlockSpec((1, d), lambda i, j: (0, 0)),
                  _mod_spec(mod, kind, shift_chunk, d, tm, rows_per_batch),
                  _mod_spec(mod, kind, scale_chunk, d, tm, rows_per_batch)],
        out_specs=pl.BlockSpec((tm, d), lambda i, j: (i, 0)),
        out_shape=jax.ShapeDtypeStruct((m, d), BF16),
        compiler_params=_params("arbitrary", "arbitrary"),
        name="modulate",
    )(x, g.reshape(1, d), mod, mod)


def _mm_res_mod_kernel(a_ref, w_ref, res_ref, gate_ref, g_ref, shift_ref, scale_ref, x_ref, h_ref):
    x = res_ref[...] + gate_ref[...] * _dot(a_ref[...], w_ref[...])
    x_ref[...] = x
    h_ref[...] = (_rms(x) * g_ref[...] * (1.0 + scale_ref[...]) + shift_ref[...]).astype(h_ref.dtype)


def _mm_res_mod(a, w, res, mod, kind, gate_chunk, g, shift_chunk, scale_chunk, *, tm, rows_per_batch,
                into=None, row0=0, total_rows=None):
    m, k = a.shape
    n = w.shape[-1]
    h_rows = m if total_rows is None else total_rows
    blk0 = row0 // tm
    assert row0 % tm == 0
    row = pl.BlockSpec((tm, n), lambda i, j: (i, 0))
    mspec = lambda chunk: _mod_spec(mod, kind, chunk, n, tm, rows_per_batch)
    in_specs = [pl.BlockSpec((tm, k), lambda i, j: (i, 0)),
                _w_spec(w, n, lambda i, j: 0),
                row, mspec(gate_chunk),
                pl.BlockSpec((1, n), lambda i, j: (0, 0)), mspec(shift_chunk), mspec(scale_chunk)]
    args = [a, w, res, mod, g.reshape(1, n), mod, mod]
    aliases = {}
    body = _mm_res_mod_kernel
    if into is not None:
        in_specs.append(pl.BlockSpec(memory_space=pl.ANY))
        args.append(into)
        aliases = {len(args) - 1: 1}
        body = lambda *refs: _mm_res_mod_kernel(*refs[:7], *refs[8:])
    return pl.pallas_call(
        body,
        grid=(m // tm, 1),
        in_specs=in_specs,
        out_specs=[row, pl.BlockSpec((tm, n), lambda i, j: (blk0 + i, 0))],
        out_shape=[jax.ShapeDtypeStruct((m, n), F32), jax.ShapeDtypeStruct((h_rows, n), BF16)],
        input_output_aliases=aliases,
        compiler_params=_params("arbitrary", "arbitrary"),
        name="mm_residual_modulate",
    )(*args)


def _mm_kernel(x_ref, w_ref, o_ref, *, act):
    acc = _dot(x_ref[...], w_ref[...])
    if act == "sigmoid":
        acc = _sigmoid(acc)
    o_ref[...] = acc.astype(o_ref.dtype)


def _w_spec(w, tn, col_of):
    lead = w.ndim - 2
    assert all(s == 1 for s in w.shape[:lead])
    return pl.BlockSpec((None,) * lead + (w.shape[-2], tn), lambda *idx: (0,) * lead + (0, col_of(*idx)))


def _mm(x, w, *, col0=0, n_out=None, tm, tn, act=None, out_dtype=F32, name="mm"):
    m, k = x.shape
    n = w.shape[-1] if n_out is None else n_out
    return pl.pallas_call(
        functools.partial(_mm_kernel, act=act),
        grid=(n // tn, m // tm),
        in_specs=[pl.BlockSpec((tm, k), lambda j, i: (i, 0)),
                  _w_spec(w, tn, lambda j, i: col0 // tn + j)],
        out_specs=pl.BlockSpec((tm, tn), lambda j, i: (i, j)),
        out_shape=jax.ShapeDtypeStruct((m, n), out_dtype),
        compiler_params=_params("arbitrary", "arbitrary"),
        name=name,
    )(x, w)


def _mm_rope_kernel(x_ref, w_ref, cos_ref, sin_ref, o_ref, *, tn, rope_tiles):
    acc = _dot(x_ref[...], w_ref[...])

    @pl.when(pl.program_id(0) < rope_tiles)
    def _():
        cos = cos_ref[...]
        sin = sin_ref[...]
        for s in range(tn // NSA_HEAD_DIM):
            sl = slice(s * NSA_HEAD_DIM, (s + 1) * NSA_HEAD_DIM)
            a = acc[:, sl]
            o_ref[:, sl] = (a * cos + pltpu.roll(a, NSA_HEAD_DIM // 2, 1) * sin).astype(o_ref.dtype)

    @pl.when(pl.program_id(0) >= rope_tiles)
    def _():
        o_ref[...] = acc.astype(o_ref.dtype)


def _mm_rope(x, w, cos, sin, *, col0, n_out, rope_cols, tm, tn, out_dtype, name):
    m, k = x.shape
    pos_tiles = cos.shape[0] // tm
    return pl.pallas_call(
        functools.partial(_mm_rope_kernel, tn=tn, rope_tiles=rope_cols // tn),
        grid=(n_out // tn, m // tm),
        in_specs=[pl.BlockSpec((tm, k), lambda j, i: (i, 0)),
                  _w_spec(w, tn, lambda j, i: col0 // tn + j),
                  pl.BlockSpec((tm, NSA_HEAD_DIM), lambda j, i: (i % pos_tiles, 0)),
                  pl.BlockSpec((tm, NSA_HEAD_DIM), lambda j, i: (i % pos_tiles, 0))],
        out_specs=pl.BlockSpec((tm, tn), lambda j, i: (i, j)),
        out_shape=jax.ShapeDtypeStruct((m, n_out), out_dtype),
        compiler_params=_params("arbitrary", "arbitrary"),
        name=name,
    )(x, w, cos, sin)


def _kv_proj_kernel(x_ref, w_ref, cos_ref, sin_ref, *o_refs, row_major):
    acc = _dot(x_ref[...], w_ref[...])
    cos = cos_ref[...]
    sin = sin_ref[...]
    hd = NSA_HEAD_DIM
    tm = acc.shape[0]
    for slot in range(KV_SLOTS):
        a = acc[:, slot * hd:(slot + 1) * hd]
        if slot < NSA_KV_GROUPS:
            a = a * cos + pltpu.roll(a, hd // 2, 1) * sin
        o_refs[0][pl.ds(slot, tm, stride=KV_SLOTS), :] = a
        if row_major:
            o_refs[1][:, slot * hd:(slot + 1) * hd] = a


def _kv_proj(x, w, cos, sin, *, col0, tm, row_major, name):
    m, k = x.shape
    hd = NSA_HEAD_DIM
    kvw2 = KV_SLOTS * hd
    pos_tiles = cos.shape[0] // tm
    out_specs = [pl.BlockSpec((tm * KV_SLOTS, hd), lambda i: (i, 0))]
    out_shape = [jax.ShapeDtypeStruct((m * KV_SLOTS, hd), F32)]
    if row_major:
        out_specs.append(pl.BlockSpec((tm, kvw2), lambda i: (i, 0)))
        out_shape.append(jax.ShapeDtypeStruct((m, kvw2), F32))
    return pl.pallas_call(
        functools.partial(_kv_proj_kernel, row_major=row_major),
        grid=(m // tm,),
        in_specs=[pl.BlockSpec((tm, k), lambda i: (i, 0)),
                  _w_spec(w, kvw2, lambda i: col0 // kvw2),
                  pl.BlockSpec((tm, hd), lambda i: (i % pos_tiles, 0)),
                  pl.BlockSpec((tm, hd), lambda i: (i % pos_tiles, 0))],
        out_specs=out_specs,
        out_shape=out_shape,
        compiler_params=_params("arbitrary"),
        name=name,
    )(x, w, cos, sin)


def _mm_res_kernel(a_ref, w_ref, res_ref, gate_ref, o_ref):
    o_ref[...] = res_ref[...] + gate_ref[...] * _dot(a_ref[...], w_ref[...])


def _mm_res(a, w, res, mod, kind, gate_chunk, *, tm, tn, rows_per_batch):
    m, k = a.shape
    n = w.shape[-1]
    tn = _row_tile(n, tn)
    swap = lambda f: (lambda j, i: f(i, j))
    gspec = _mod_spec(mod, kind, gate_chunk, tn, tm, rows_per_batch)
    gspec = pl.BlockSpec(gspec.block_shape, swap(gspec.index_map))
    return pl.pallas_call(
        _mm_res_kernel,
        grid=(n // tn, m // tm),
        in_specs=[pl.BlockSpec((tm, k), lambda j, i: (i, 0)),
                  _w_spec(w, tn, lambda j, i: j),
                  pl.BlockSpec((tm, tn), lambda j, i: (i, j)),
                  gspec],
        out_specs=pl.BlockSpec((tm, tn), lambda j, i: (i, j)),
        out_shape=jax.ShapeDtypeStruct((m, n), F32),
        compiler_params=_params("arbitrary", "arbitrary"),
        name="mm_residual",
    )(a, w, res, mod)


def _swiglu_up_kernel(x_ref, wg_ref, wu_ref, o_ref):
    x = x_ref[...]
    o_ref[...] = (_silu(_dot(x, wg_ref[...])) * _dot(x, wu_ref[...])).astype(o_ref.dtype)


def _swiglu_up(x, wg, wu, *, tm, tn):
    m, k = x.shape
    f = wg.shape[-1]
    return pl.pallas_call(
        _swiglu_up_kernel,
        grid=(pl.cdiv(f, tn), m // tm),
        in_specs=[pl.BlockSpec((tm, k), lambda j, i: (i, 0)),
                  _w_spec(wg, tn, lambda j, i: j),
                  _w_spec(wu, tn, lambda j, i: j)],
        out_specs=pl.BlockSpec((tm, tn), lambda j, i: (i, j)),
        out_shape=jax.ShapeDtypeStruct((m, f), BF16),
        compiler_params=_params("arbitrary", "arbitrary"),
        name="swiglu_up",
    )(x, wg, wu)


def _grouped_kernel(meta_ref, *refs, body, n_blk):
    @pl.when(pl.program_id(1) < meta_ref[n_blk])
    def _():
        body(*refs)


def _grouped_specs(n_blk, tm, k_in, tn, weights):
    row = lambda r, meta: jnp.minimum(r, meta[n_blk] - 1)
    x_spec = pl.BlockSpec((tm, k_in), lambda j, r, meta: (row(r, meta), 0))
    w_specs = [pl.BlockSpec((None, None, w.shape[-2], tn), lambda j, r, meta: (0, meta[row(r, meta)], 0, j))
               for w in weights]
    o_spec = pl.BlockSpec((tm, tn), lambda j, r, meta: (row(r, meta), j))
    return [x_spec] + w_specs, o_spec


def _moe_up(xb, meta, wg, wu, *, tm, tn):
    rows, k = xb.shape
    f = wg.shape[-1]
    tn = _row_tile(f, tn)
    n_blk = rows // tm
    in_specs, o_spec = _grouped_specs(n_blk, tm, k, tn, [wg, wu])
    grid_spec = pltpu.PrefetchScalarGridSpec(
        num_scalar_prefetch=1, grid=(f // tn, n_blk), in_specs=in_specs, out_specs=o_spec)
    return pl.pallas_call(
        functools.partial(_grouped_kernel, body=_swiglu_up_kernel, n_blk=n_blk),
        grid_spec=grid_spec,
        out_shape=jax.ShapeDtypeStruct((rows, f), BF16),
        compiler_params=_params("arbitrary", "arbitrary"),
        name="moe_up",
    )(meta, xb, wg, wu)


def _mm_plain_kernel(a_ref, w_ref, o_ref):
    o_ref[...] = _dot(a_ref[...], w_ref[...])


def _moe_down(act, meta, wd, *, tm, tn):
    rows, f = act.shape
    d = wd.shape[-1]
    tn = _row_tile(d, tn)
    n_blk = rows // tm
    in_specs, o_spec = _grouped_specs(n_blk, tm, f, tn, [wd])
    grid_spec = pltpu.PrefetchScalarGridSpec(
        num_scalar_prefetch=1, grid=(d // tn, n_blk), in_specs=in_specs, out_specs=o_spec)
    return pl.pallas_call(
        functools.partial(_grouped_kernel, body=_mm_plain_kernel, n_blk=n_blk),
        grid_spec=grid_spec,
        out_shape=jax.ShapeDtypeStruct((rows, d), F32),
        compiler_params=_params("arbitrary", "arbitrary"),
        name="moe_down",
    )(meta, act, wd)


def _router_kernel(x_ref, w_ref, o_ref):
    lg = _dot(x_ref[...], w_ref[...])
    lane = lax.broadcasted_iota(jnp.int32, lg.shape, 1)
    x = jnp.where(lane < N_EXPERTS, lg, -jnp.inf)
    v0 = jnp.max(x, axis=-1, keepdims=True)
    i0 = jnp.min(jnp.where(x == v0, lane, LANES), axis=-1, keepdims=True)
    x1 = jnp.where(lane == i0, -jnp.inf, x)
    v1 = jnp.max(x1, axis=-1, keepdims=True)
    i1 = jnp.min(jnp.where(x1 == v1, lane, LANES), axis=-1, keepdims=True)
    e1 = jnp.exp(v1 - v0)
    g0 = 1.0 / (1.0 + e1)
    g1 = e1 / (1.0 + e1)
    o_ref[...] = jnp.where(lane == 0, g0,
                           jnp.where(lane == 1, g1,
                                     jnp.where(lane == 2, i0.astype(F32),
                                               jnp.where(lane == 3, i1.astype(F32), 0.0))))


def _router(h, w_router, tm):
    n, d = h.shape
    w_pad = jnp.pad(w_router, ((0, 0), (0, LANES - N_EXPERTS)))
    out = pl.pallas_call(
        _router_kernel,
        grid=(n // tm,),
        in_specs=[pl.BlockSpec((tm, d), lambda i: (i, 0)),
                  pl.BlockSpec((d, LANES), lambda i: (0, 0))],
        out_specs=pl.BlockSpec((tm, LANES), lambda i: (i, 0)),
        out_shape=jax.ShapeDtypeStruct((n, LANES), F32),
        compiler_params=_params("arbitrary"),
        name="moe_router",
    )(h, w_pad)
    return out[:, :TOP_K], out[:, TOP_K:2 * TOP_K].astype(jnp.int32)


def _combine_norm_kernel(x_ref, y0_ref, y1_ref, w_ref, gate_ref, g_ref, o_ref):
    w = w_ref[...]
    y = w[:, 0:1] * y0_ref[...] + w[:, 1:2] * y1_ref[...]
    xo = x_ref[...] + gate_ref[...] * y
    o_ref[...] = _rms(xo) * g_ref[...]


def _combine_norm(x, y0, y1, wts, mod, kind, gate_chunk, g, *, tm, rows_per_batch):
    m, d = x.shape
    row = pl.BlockSpec((tm, d), lambda i, j: (i, 0))
    return pl.pallas_call(
        _combine_norm_kernel,
        grid=(m // tm, 1),
        in_specs=[row, row, row,
                  pl.BlockSpec((tm, TOP_K), lambda i, j: (i, 0)),
                  _mod_spec(mod, kind, gate_chunk, d, tm, rows_per_batch),
                  pl.BlockSpec((1, d), lambda i, j: (0, 0))],
        out_specs=row,
        out_shape=jax.ShapeDtypeStruct((m, d), F32),
        compiler_params=_params("arbitrary", "arbitrary"),
        name="moe_combine_final_norm",
    )(x, y0, y1, wts, mod, g.reshape(1, d))


KV_SLOTS = 2 * NSA_KV_GROUPS


def _slot_rows(ref, slot, n_tok, first_tok=0):
    return ref[pl.ds(first_tok * KV_SLOTS + slot, n_tok, stride=KV_SLOTS), :]


def _compress_native_kernel(*refs, n_in):
    w_ref, o_ref = refs[n_in], refs[n_in + 1]
    hd = NSA_HEAD_DIM
    row = 0
    for x_ref in refs[:n_in]:
        n_tok = x_ref.shape[0] // KV_SLOTS
        nb = n_tok // NSA_BLOCK
        for slot in range(KV_SLOTS):
            x = _slot_rows(x_ref, slot, n_tok)
            w = w_ref[slot // NSA_KV_GROUPS]
            pooled = jnp.sum(x.reshape(nb, NSA_BLOCK, hd) * w[None], axis=1)
            o_ref[row:row + nb, slot * hd:(slot + 1) * hd] = pooled
        row += nb


def _compress_pages_kernel(pt_ref, *refs, n_in):
    del pt_ref
    _compress_native_kernel(*refs, n_in=n_in)


def _compress_tokens(kv, w_pool, tokens_per_step):
    b, rows, hd = kv.shape
    t = rows // KV_SLOTS
    nb = tokens_per_step // NSA_BLOCK
    return pl.pallas_call(
        functools.partial(_compress_native_kernel, n_in=1),
        grid=(b, t // tokens_per_step),
        in_specs=[pl.BlockSpec((None, tokens_per_step * KV_SLOTS, hd), lambda i, s: (i, s, 0)),
                  pl.BlockSpec(w_pool.shape, lambda i, s: (0, 0, 0))],
        out_specs=pl.BlockSpec((None, nb, KV_SLOTS * hd), lambda i, s: (i, s, 0)),
        out_shape=jax.ShapeDtypeStruct((b, t // NSA_BLOCK, KV_SLOTS * hd), F32),
        compiler_params=_params("arbitrary", "arbitrary"),
        name="nsa_compress_prompt",
    )(kv, w_pool)


def _compress_pages(cache, page_table, w_pool, pages_per_step):
    bsz, n_pages = page_table.shape
    rows, hd = cache.shape[1:]
    p = pages_per_step
    bpp = PAGE_SIZE // NSA_BLOCK
    kvw = KV_SLOTS * hd

    def page_spec(k):
        return pl.BlockSpec((None, rows, hd), lambda i, s, pt: (pt[i, s * p + k], 0, 0))

    grid_spec = pltpu.PrefetchScalarGridSpec(
        num_scalar_prefetch=1,
        grid=(bsz, n_pages // p),
        in_specs=[page_spec(k) for k in range(p)] + [pl.BlockSpec(w_pool.shape, lambda i, s, pt: (0, 0, 0))],
        out_specs=pl.BlockSpec((None, p * bpp, kvw), lambda i, s, pt: (i, s, 0)))
    return pl.pallas_call(
        functools.partial(_compress_pages_kernel, n_in=p),
        grid_spec=grid_spec,
        out_shape=jax.ShapeDtypeStruct((bsz, n_pages * bpp, kvw), F32),
        compiler_params=_params("arbitrary", "arbitrary"),
        name="nsa_compress_pages",
    )(page_table, *([cache] * p), w_pool)


def _select_blocks(score, n_valid):
    rows, width = score.shape
    col = lax.broadcasted_iota(jnp.int32, (rows, width), 1)
    rank = jnp.zeros((rows, width), F32)
    for m in range(n_valid):
        c = score[:, m:m + 1]
        beats = jnp.where(c > score, 1.0, jnp.where((c == score) & (col > m), 1.0, 0.0))
        rank = rank + beats
    return jnp.where((rank < NSA_TOP_BLOCKS) & (col < n_valid), 1.0, 0.0)


def _softmax_rows(s):
    e = jnp.exp(s - jnp.max(s, axis=-1, keepdims=True))
    return e, jnp.sum(e, axis=-1, keepdims=True)


def _compressed_branch(q4, ck, cv, tpos, reps):
    scale = NSA_HEAD_DIM ** -0.5
    nc = ck.shape[0]
    tq = q4.shape[0] // reps
    s = _dot_nt(q4, ck) * scale
    blk = lax.broadcasted_iota(jnp.int32, (1, nc), 1)
    ok = ((blk + 1) * NSA_BLOCK - 1) <= tpos
    e, l = _softmax_rows(jnp.where(ok, s, NEG_INF))
    p = jnp.where(ok, e / l, 0.0)
    o_c = _dot(p, cv)
    imp = p[0:tq]
    for r in range(1, reps):
        imp = imp + p[r * tq:(r + 1) * tq]
    return o_c, imp


def _compressed_select_t(q4, ck, cv, qpos_row, reps):
    scale = NSA_HEAD_DIM ** -0.5
    nc = ck.shape[0]
    tq = q4.shape[0] // reps
    tpos = jnp.concatenate([qpos_row] * reps, axis=1)
    blk = lax.broadcasted_iota(jnp.int32, (nc, 1), 0)
    s = _dot_nt(ck, q4) * scale
    ok = ((blk + 1) * NSA_BLOCK - 1) <= tpos
    s = jnp.where(ok, s, NEG_INF)
    e = jnp.exp(s - jnp.max(s, axis=0, keepdims=True))
    p = jnp.where(ok, e / jnp.sum(e, axis=0, keepdims=True), 0.0)
    o_c = lax.dot_general(p.astype(BF16), cv.astype(BF16), _TN, preferred_element_type=F32)
    imp = p[:, 0:tq]
    for r in range(1, reps):
        imp = imp + p[:, r * tq:(r + 1) * tq]
    cur = qpos_row // NSA_BLOCK
    forced = (blk == 0) | (blk == cur) | (blk == cur - 1)
    score = jnp.where(forced, FORCE_SCORE, jnp.where(blk <= cur, imp, -1.0))
    row = lax.broadcasted_iota(jnp.int32, (nc, tq), 0)
    rank = jnp.zeros((nc, tq), F32)
    for m in range(nc):
        c = score[m:m + 1, :]
        rank = rank + jnp.where(c > score, 1.0, jnp.where((c == score) & (row > m), 1.0, 0.0))
    return o_c, jnp.where(rank < NSA_TOP_BLOCKS, 1.0, 0.0)


def _block_scores(imp, qpos, n_blocks_pad):
    tq, n_imp = imp.shape
    if n_blocks_pad > n_imp:
        imp = jnp.concatenate([imp, jnp.zeros((tq, n_blocks_pad - n_imp), F32)], axis=1)
    blk = lax.broadcasted_iota(jnp.int32, (1, n_blocks_pad), 1)
    cur = qpos // NSA_BLOCK
    forced = (blk == 0) | (blk == cur) | (blk == cur - 1)
    return jnp.where(forced, FORCE_SCORE, jnp.where(blk <= cur, imp, -1.0))


def _nsa_prompt_kernel(q_ref, ck_ref, cv_ref, ks_ref, vs_ref, kw_ref, vw_ref, gate_ref, o_ref, *, tq, kc, seq):
    hd, reps = NSA_HEAD_DIM, NSA_REP
    c2 = (hd ** -0.5) * LOG2_E
    q0 = pl.program_id(2) * tq
    qf = q_ref[...]
    q4 = jnp.concatenate([qf[:, r * hd:(r + 1) * hd] for r in range(reps)], axis=0).astype(BF16)
    qpos = q0 + lax.broadcasted_iota(jnp.int32, (tq, 1), 0)
    m_rows = reps * tq
    stack = lambda x: jnp.concatenate([x] * reps, axis=0)

    n_blocks = seq // NSA_BLOCK
    qpos_row = q0 + lax.broadcasted_iota(jnp.int32, (1, tq), 1)
    o_c, sel = _compressed_select_t(q4, ck_ref[...], cv_ref[...], qpos_row, reps)
    sel = sel.astype(BF16)

    blk_row = lax.broadcasted_iota(jnp.int32, (n_blocks, 1), 0)

    def chunk(c, carry):
        m_i, l_i, acc = carry
        k0 = pl.multiple_of(c * kc, kc)
        kk = ks_ref[pl.ds(k0, kc), :]
        vv = vs_ref[pl.ds(k0, kc), :]
        kpos = k0 + lax.broadcasted_iota(jnp.int32, (1, kc), 1)
        expand = jnp.where(kpos // NSA_BLOCK == blk_row, 1.0, 0.0).astype(BF16)
        selk = lax.dot_general(sel, expand, _TN, preferred_element_type=F32)
        bias = jnp.where(kpos <= qpos, (1.0 - selk) * NEG_INF, NEG_INF)
        s = _dot_nt(q4, kk) * c2 + stack(bias)
        m_new = jnp.maximum(m_i, jnp.max(s, axis=-1, keepdims=True))
        alpha = jnp.exp2(m_i - m_new)
        p = jnp.exp2(s - m_new)
        l_new = alpha * l_i + jnp.sum(p, axis=-1, keepdims=True)
        return m_new, l_new, alpha * acc + _dot(p, vv)

    n_chunks = (q0 + tq + kc - 1) // kc
    init = (jnp.full((m_rows, 1), NEG_INF, F32), jnp.zeros((m_rows, 1), F32), jnp.zeros((m_rows, hd), F32))
    _, l_s, acc_s = lax.fori_loop(0, n_chunks, chunk, init)
    o_s = acc_s / l_s

    span = NSA_WINDOW + tq
    w0 = pl.multiple_of(jnp.maximum(q0 - NSA_WINDOW, 0), tq)
    dist = qpos - (w0 + lax.broadcasted_iota(jnp.int32, (1, span), 1))
    bias = jnp.where(dist >= 0, jnp.where(dist < NSA_WINDOW, 0.0, NEG_INF), NEG_INF)
    s = _dot_nt(q4, kw_ref[pl.ds(w0, span), :]) * c2 + stack(bias)
    e = jnp.exp2(s - jnp.max(s, axis=-1, keepdims=True))
    o_w = _dot(e, vw_ref[pl.ds(w0, span), :]) / jnp.sum(e, axis=-1, keepdims=True)

    gate = gate_ref[...]
    for r in range(reps):
        rows = slice(r * tq, (r + 1) * tq)
        o = (gate[:, 3 * r:3 * r + 1] * o_c[rows] + gate[:, 3 * r + 1:3 * r + 2] * o_s[rows]
             + gate[:, 3 * r + 2:3 * r + 3] * o_w[rows])
        o_ref[:, r * hd:(r + 1) * hd] = o.astype(o_ref.dtype)


def _nsa_prompt_attention(q, ckv, kv_s, kv_w, gates, *, tq, kc):
    b, t, _ = q.shape
    g_, hd = NSA_KV_GROUPS, NSA_HEAD_DIM
    qw = NSA_REP * hd
    nc = ckv.shape[1]
    keys = pl.BlockSpec((None, t, hd), lambda i, g, s: (i, 0, g))
    vals = pl.BlockSpec((None, t, hd), lambda i, g, s: (i, 0, g_ + g))
    return pl.pallas_call(
        functools.partial(_nsa_prompt_kernel, tq=tq, kc=kc, seq=t),
        grid=(b, g_, t // tq),
        in_specs=[pl.BlockSpec((None, tq, qw), lambda i, g, s: (i, s, g)),
                  pl.BlockSpec((None, nc, hd), lambda i, g, s: (i, 0, g)),
                  pl.BlockSpec((None, nc, hd), lambda i, g, s: (i, 0, g_ + g)),
                  keys, vals, keys, vals,
                  pl.BlockSpec((None, tq, LANES), lambda i, g, s: (i, s, g))],
        out_specs=pl.BlockSpec((None, tq, qw), lambda i, g, s: (i, s, g)),
        out_shape=jax.ShapeDtypeStruct((b, t, NSA_Q_WIDTH), BF16),
        compiler_params=_params("arbitrary", "arbitrary", "arbitrary"),
        name="nsa_prompt_attention",
    )(q, ckv, ckv, kv_s, kv_s, kv_w, kv_w, gates)


def _nsa_sample_select_kernel(q_ref, ckv_ref, oc_ref, bias_ref, *, t_new, past, n_blocks, n_pad, chunk):
    hd, reps = NSA_HEAD_DIM, NSA_REP
    qpos = past + lax.broadcasted_iota(jnp.int32, (t_new, 1), 0)
    tpos = jnp.concatenate([qpos] * reps, axis=0)
    sels = []
    for g in range(NSA_KV_GROUPS):
        ck = ckv_ref[:, g * hd:(g + 1) * hd]
        cv = ckv_ref[:, NSA_KV_WIDTH + g * hd:NSA_KV_WIDTH + (g + 1) * hd]
        o_c, imp = _compressed_branch(q_ref[g].astype(BF16), ck, cv, tpos, reps)
        oc_ref[g] = o_c
        sels.append(_select_blocks(_block_scores(imp, qpos, n_pad), n_blocks).astype(BF16))
    n_keys = bias_ref.shape[-1]
    blk_row = lax.broadcasted_iota(jnp.int32, (n_pad, 1), 0)
    for k0 in range(0, n_keys, chunk):
        width = min(chunk, n_keys - k0)
        kpos = k0 + lax.broadcasted_iota(jnp.int32, (1, width), 1)
        expand = jnp.where(kpos // NSA_BLOCK == blk_row, 1.0, 0.0).astype(BF16)
        for g in range(NSA_KV_GROUPS):
            selk = jnp.dot(sels[g], expand, preferred_element_type=F32)
            bias_ref[g, :, k0:k0 + width] = jnp.where(kpos <= qpos, (1.0 - selk) * NEG_INF, NEG_INF)


def _nsa_sample_select(q4, ckv, *, t_new, past, n_blocks, n_pad, n_keys, chunk):
    b = q4.shape[0]
    g_, hd = NSA_KV_GROUPS, NSA_HEAD_DIM
    rows = NSA_REP * t_new
    nc = ckv.shape[1]
    return pl.pallas_call(
        functools.partial(_nsa_sample_select_kernel, t_new=t_new, past=past, n_blocks=n_blocks, n_pad=n_pad,
                          chunk=chunk),
        grid=(b,),
        in_specs=[pl.BlockSpec((None, g_, rows, hd), lambda i: (i, 0, 0, 0)),
                  pl.BlockSpec((None, nc, 2 * NSA_KV_WIDTH), lambda i: (i, 0, 0))],
        out_specs=[pl.BlockSpec((None, g_, rows, hd), lambda i: (i, 0, 0, 0)),
                   pl.BlockSpec((None, g_, t_new, n_keys), lambda i: (i, 0, 0, 0))],
        out_shape=[jax.ShapeDtypeStruct((b, g_, rows, hd), F32),
                   jax.ShapeDtypeStruct((b, g_, t_new, n_keys), F32)],
        compiler_params=_params("arbitrary"),
        name="nsa_sample_select",
    )(q4, ckv)


def _nsa_sample_attend_kernel(pt_ref, *refs, n_pg, t_new, past):
    del pt_ref
    pages = refs[:n_pg]
    (q_ref, bias_ref, bias_new_ref, oc_ref, new_s_ref, win_ref, new_w_ref, gate_ref,
     o_ref, win_out_ref, m_ref, l_ref, acc_ref) = refs[n_pg:]
    hd, reps, g_ = NSA_HEAD_DIM, NSA_REP, NSA_KV_GROUPS
    kvw = NSA_KV_WIDTH
    scale = hd ** -0.5
    c2 = scale * LOG2_E
    step = pl.program_id(1)
    qpos = past + lax.broadcasted_iota(jnp.int32, (t_new, 1), 0)
    tpos = jnp.concatenate([qpos] * reps, axis=0)
    stack = lambda x: jnp.concatenate([x] * reps, axis=0)

    def online(keys, vals, biases):
        groups = range(g_)
        m_i = [m_ref[g] for g in groups]
        l_i = [l_ref[g] for g in groups]
        acc_i = [acc_ref[g] for g in groups]
        s = [_dot_nt(q_ref[g], keys[g]) * c2 + stack(biases[g]) for g in groups]
        m_new = [jnp.maximum(m_i[g], jnp.max(s[g], axis=-1, keepdims=True)) for g in groups]
        alpha = [jnp.exp2(m_i[g] - m_new[g]) for g in groups]
        p = [jnp.exp2(s[g] - m_new[g]) for g in groups]
        pv = [_dot(p[g], vals[g]) for g in groups]
        for g in groups:
            l_ref[g] = alpha[g] * l_i[g] + jnp.sum(p[g], axis=-1, keepdims=True)
            acc_ref[g] = alpha[g] * acc_i[g] + pv[g]
            m_ref[g] = m_new[g]

    def new_rows(ref, c0):
        x = ref[:, c0:c0 + hd]
        return jnp.concatenate([x, jnp.zeros((PAGE_SIZE - t_new, hd), x.dtype)], axis=0)

    @pl.when(step == 0)
    def _():
        m_ref[...] = jnp.full(m_ref.shape, NEG_INF, F32)
        l_ref[...] = jnp.zeros(l_ref.shape, F32)
        acc_ref[...] = jnp.zeros(acc_ref.shape, F32)
        online([new_rows(new_s_ref, g * hd) for g in range(g_)],
               [new_rows(new_s_ref, kvw + g * hd) for g in range(g_)],
               [bias_new_ref[g] for g in range(g_)])

    page_rows = lambda slot: jnp.concatenate([_slot_rows(pg, slot, PAGE_SIZE).astype(BF16) for pg in pages], axis=0)
    online([page_rows(g) for g in range(g_)], [page_rows(g_ + g) for g in range(g_)],
           [bias_ref[g] for g in range(g_)])

    @pl.when(step == pl.num_programs(1) - 1)
    def _():
        wb = win_ref.shape[0] // KV_SLOTS
        wpos = past - wb + lax.broadcasted_iota(jnp.int32, (1, wb + PAGE_SIZE), 1)
        dist = tpos - wpos
        gate = gate_ref[...]
        for g in range(g_):
            kk = jnp.concatenate([_slot_rows(win_ref, g, wb), new_rows(new_w_ref, g * hd)], axis=0)
            vv = jnp.concatenate([_slot_rows(win_ref, g_ + g, wb), new_rows(new_w_ref, kvw + g * hd)], axis=0)
            s = _dot_nt(q_ref[g], kk) * scale
            ok = (wpos >= 0) & (dist >= 0) & (dist < NSA_WINDOW)
            e, l = _softmax_rows(jnp.where(ok, s, NEG_INF))
            o_w = _dot(e, vv) / l
            o_s = acc_ref[g] / l_ref[g]
            o_c = oc_ref[g]
            for r in range(reps):
                rs = slice(r * t_new, (r + 1) * t_new)
                c0 = g * LANES + 3 * r
                o = (gate[:, c0:c0 + 1] * o_c[rs] + gate[:, c0 + 1:c0 + 2] * o_s[rs]
                     + gate[:, c0 + 2:c0 + 3] * o_w[rs])
                col = (g * reps + r) * hd
                o_ref[:, col:col + hd] = o.astype(o_ref.dtype)
        keep = (wb - t_new) * KV_SLOTS
        win_out_ref[0:keep, :] = win_ref[t_new * KV_SLOTS:wb * KV_SLOTS, :]
        for slot in range(KV_SLOTS):
            win_out_ref[pl.ds(keep + slot, t_new, stride=KV_SLOTS), :] = new_w_ref[:, slot * hd:(slot + 1) * hd]


def _nsa_sample_attend(q4, bias, o_c, kv_s, win, kv_w, gates, cache, page_table, *, n_pg, past):
    b, g_, rows, hd = q4.shape
    t_new = rows // NSA_REP
    width = n_pg * PAGE_SIZE
    kvw2 = 2 * NSA_KV_WIDTH
    n_pages = page_table.shape[1]
    win_rows = win.shape[1]
    page_rows = cache.shape[1]

    def page_spec(k):
        return pl.BlockSpec((None, page_rows, hd), lambda i, s, pt: (pt[i, s * n_pg + k], 0, 0))

    per_b = lambda shape: pl.BlockSpec((None,) + shape, lambda i, s, pt: (i,) + (0,) * len(shape))
    grid_spec = pltpu.PrefetchScalarGridSpec(
        num_scalar_prefetch=1,
        grid=(b, n_pages // n_pg),
        in_specs=[page_spec(k) for k in range(n_pg)] + [
            per_b((g_, rows, hd)),
            pl.BlockSpec((None, g_, t_new, width), lambda i, s, pt: (i, 0, 0, s)),
            pl.BlockSpec((None, g_, t_new, PAGE_SIZE), lambda i, s, pt: (i, 0, 0, past // PAGE_SIZE)),
            per_b((g_, rows, hd)),
            per_b((t_new, kvw2)), per_b((win_rows, hd)), per_b((t_new, kvw2)),
            per_b((t_new, g_ * LANES))],
        out_specs=[per_b((t_new, NSA_Q_WIDTH)), per_b((win_rows, hd))],
        scratch_shapes=[pltpu.VMEM((g_, rows, 1), F32), pltpu.VMEM((g_, rows, 1), F32),
                        pltpu.VMEM((g_, rows, hd), F32)])
    return pl.pallas_call(
        functools.partial(_nsa_sample_attend_kernel, n_pg=n_pg, t_new=t_new, past=past),
        grid_spec=grid_spec,
        out_shape=[jax.ShapeDtypeStruct((b, t_new, NSA_Q_WIDTH), BF16),
                   jax.ShapeDtypeStruct((b, win_rows, hd), F32)],
        compiler_params=_params("arbitrary", "arbitrary"),
        name="nsa_sample_attention",
    )(page_table, *([cache] * n_pg), q4, bias, bias, o_c, kv_s, win, kv_w, gates)


def _log_sigmoid(x):
    return jnp.minimum(x, 0.0) - jnp.log1p(jnp.exp(-jnp.abs(x)))


def _mlstm_kernel(q_ref, k_ref, v_ref, o_ref, gc_ref, gr_ref, bc_ref, br_ref, ng_ref,
                  c0_ref, n0_ref, m0_ref, y_ref, c_out, n_out, m_out, c_s, n_s, m_s):
    nh, dk, dv = MLSTM_HEADS, MLSTM_QK_DIM, MLSTM_V_DIM
    step = pl.program_id(1)
    ln = q_ref.shape[0]

    @pl.when(step == 0)
    def _():
        c_s[...] = c0_ref[...]
        n_s[...] = n0_ref[...]
        m_s[...] = m0_ref[...]

    hi = lax.Precision.HIGHEST
    r_i = lax.broadcasted_iota(jnp.int32, (ln, ln), 0)
    c_i = lax.broadcasted_iota(jnp.int32, (ln, ln), 1)
    causal = c_i <= r_i
    lower = jnp.where(causal, 1.0, 0.0)
    upper = jnp.where(r_i <= c_i, 1.0, 0.0)
    gcol = gc_ref[...] + bc_ref[...]
    grow = gr_ref[...] + br_ref[...]
    ig_c = gcol[:, :nh]
    b_c = jnp.dot(lower, _log_sigmoid(gcol[:, nh:]), precision=hi, preferred_element_type=F32)
    ig_r = grow[:nh, :]
    b_r = jnp.dot(_log_sigmoid(grow[nh:, :]), upper, precision=hi, preferred_element_type=F32)

    heads = range(nh)
    q = [q_ref[:, h * dk:(h + 1) * dk] for h in heads]
    k = [k_ref[:, h * dk:(h + 1) * dk] * (dk ** -0.5) for h in heads]
    vb = [v_ref[:, h * dv:(h + 1) * dv].astype(BF16) for h in heads]
    qb = [x.astype(BF16) for x in q]
    c_prev = [c_s[h] for h in heads]
    n_prev = [n_s[h] for h in heads]
    m_prev = [m_s[h] for h in heads]
    bt = [b_c[:, h:h + 1] for h in heads]
    b_last = [b_c[ln - 1:ln, h:h + 1] for h in heads]

    d = [jnp.where(causal, bt[h] - b_r[h:h + 1, :] + ig_r[h:h + 1, :], -jnp.inf) for h in heads]
    inter = [bt[h] + m_prev[h] for h in heads]
    m_t = [jnp.maximum(inter[h], jnp.max(d[h], axis=-1, keepdims=True)) for h in heads]
    a = [jnp.exp(inter[h] - m_t[h]) for h in heads]
    m_new = [m_t[h][ln - 1:ln, :] for h in heads]
    ws = [jnp.exp(b_last[h] - bt[h] + ig_c[:, h:h + 1] - m_new[h]) for h in heads]
    decay = [jnp.exp(b_last[h] + m_prev[h] - m_new[h]) for h in heads]
    kw = [k[h] * ws[h] for h in heads]

    qk = [_dot_nt(qb[h], k[h]) for h in heads]
    qc = [_dot(qb[h], c_prev[h]) for h in heads]
    kwv = [lax.dot_general(kw[h].astype(BF16), vb[h], _TN, preferred_element_type=F32) for h in heads]
    w = [jnp.exp(d[h] - m_t[h]) * qk[h] for h in heads]
    wv = [_dot(w[h], vb[h]) for h in heads]

    ys = []
    for h in heads:
        num = a[h] * qc[h] + wv[h]
        den = a[h] * jnp.sum(q[h] * n_prev[h], axis=-1, keepdims=True) + jnp.sum(w[h], axis=-1, keepdims=True)
        hh = num / jnp.maximum(jnp.abs(den), jnp.exp(-m_t[h]))
        hn = _rms(hh) * ng_ref[:, h * dv:(h + 1) * dv]
        ys.append((hn * _sigmoid(o_ref[:, h * dv:(h + 1) * dv])).astype(y_ref.dtype))

    for h in heads:
        y_ref[:, h * dv:(h + 1) * dv] = ys[h]
        c_s[h] = decay[h] * c_prev[h] + kwv[h]
        n_s[h] = decay[h] * n_prev[h] + jnp.sum(kw[h], axis=0, keepdims=True)
        m_s[h] = m_new[h]

    @pl.when(step == pl.num_programs(1) - 1)
    def _():
        c_out[...] = c_s[...]
        n_out[...] = n_s[...]
        m_out[...] = m_s[...]


def _mlstm(proj, graw, b_gate, norm_g, c0, n0, m0, *, chunk):
    b, t, _ = proj.shape
    nh, dk, dv = MLSTM_HEADS, MLSTM_QK_DIM, MLSTM_V_DIM
    nchunk = t // chunk
    qw, vw = nh * dk, nh * dv
    grow = jnp.transpose(graw.reshape(b, nchunk, chunk, 2 * nh), (0, 1, 3, 2))
    state = lambda shape: pl.BlockSpec((None,) + shape, lambda i, s: (i,) + (0,) * len(shape))
    const = lambda shape: pl.BlockSpec(shape, lambda i, s: (0,) * len(shape))
    return pl.pallas_call(
        _mlstm_kernel,
        grid=(b, nchunk),
        in_specs=[pl.BlockSpec((None, chunk, qw), lambda i, s: (i, s, 0)),
                  pl.BlockSpec((None, chunk, qw), lambda i, s: (i, s, 1)),
                  pl.BlockSpec((None, chunk, vw), lambda i, s: (i, s, 2 * qw // vw)),
                  pl.BlockSpec((None, chunk, vw), lambda i, s: (i, s, 2 * qw // vw + 1)),
                  pl.BlockSpec((None, chunk, 2 * nh), lambda i, s: (i, s, 0)),
                  pl.BlockSpec((None, None, 2 * nh, chunk), lambda i, s: (i, s, 0, 0)),
                  const((1, 2 * nh)), const((2 * nh, 1)), const((1, vw)),
                  state((nh, dk, dv)), state((nh, 1, dk)), state((nh, 1, 1))],
        out_specs=[pl.BlockSpec((None, chunk, vw), lambda i, s: (i, s, 0)),
                   state((nh, dk, dv)), state((nh, 1, dk)), state((nh, 1, 1))],
        out_shape=[jax.ShapeDtypeStruct((b, t, vw), BF16),
                   jax.ShapeDtypeStruct((b, nh, dk, dv), F32),
                   jax.ShapeDtypeStruct((b, nh, 1, dk), F32),
                   jax.ShapeDtypeStruct((b, nh, 1, 1), F32)],
        scratch_shapes=[pltpu.VMEM((nh, dk, dv), F32), pltpu.VMEM((nh, 1, dk), F32),
                        pltpu.VMEM((nh, 1, 1), F32)],
        compiler_params=_params("arbitrary", "arbitrary"),
        name="mlstm",
    )(proj, proj, proj, proj, graw, grow, b_gate.reshape(1, 2 * nh), b_gate.reshape(2 * nh, 1),
      norm_g.reshape(1, vw), c0, n0.reshape(b, nh, 1, dk), m0.reshape(b, nh, 1, 1))


def _rope_tables(pos):
    half = NSA_HEAD_DIM // 2
    freq = ROPE_THETA ** (-jnp.arange(half, dtype=F32) / half)
    ang = pos.astype(F32)[:, None] * freq[None, :]
    cos, sin = jnp.cos(ang), jnp.sin(ang)
    return jnp.concatenate([cos, cos], axis=-1), jnp.concatenate([-sin, sin], axis=-1)


def _gate_weights(w_in):
    d = w_in.shape[0]
    per_group = NSA_REP * 3
    w = w_in[:, NSA_Q_WIDTH + 6 * NSA_KV_WIDTH:].reshape(d, NSA_KV_GROUPS, per_group)
    w = jnp.pad(w, ((0, 0), (0, 0), (0, LANES - per_group)))
    return w.reshape(d, NSA_KV_GROUPS * LANES)


def _row_tile(m, pref):
    t = min(pref, m)
    while m % t:
        t //= 2
    return t


class _Group:
    def __init__(self, x, mod_rows, kind):
        self.b, self.t, self.d = x.shape
        self.x = x.reshape(self.b * self.t, self.d)
        self.kind = kind
        self.mod_rows = mod_rows

    def mod(self, layer):
        m = self.mod_rows[layer]
        if self.kind == "batch":
            return m.reshape(self.b, 1, m.shape[-1])
        return jnp.repeat(m, self.t, axis=0)


def kernel(x_prompt, x_sample, c_prompt, c_sample, cache_nsa_cmp_kv, cache_nsa_sel_kv, page_table, state_nsa_win_kv, state_mlstm_c, state_mlstm_n, state_mlstm_m, ada_w, ada_b, norm_mix_g, norm_ffn_g, norm_final_g, nsa_w_in, nsa_w_pool, nsa_w_out, mlstm_w_in, mlstm_b_gate, mlstm_norm_g, mlstm_w_out, ffn_w_gate, ffn_w_up, ffn_w_down, moe_w_router, moe_w_gate, moe_w_up, moe_w_down):
    bp, tp, d = x_prompt.shape
    bs, ts, _ = x_sample.shape
    past = page_table.shape[1] * PAGE_SIZE
    assert tp % NSA_BLOCK == 0 and tp >= NSA_WINDOW + 128 and ts < NSA_BLOCK and ts % SUBLANES == 0
    assert past % NSA_BLOCK == 0 and state_nsa_win_kv.shape[2] == NSA_WINDOW
    assert ada_w.shape[0] == 2 and nsa_w_in.shape[0] == 1 and mlstm_w_in.shape[0] == 1

    mod_all = _adaln(jnp.concatenate([c_prompt, c_sample], axis=0), ada_w, ada_b)
    groups = [_Group(x_prompt, mod_all[:, :bp], "batch"), _Group(x_sample, mod_all[:, bp:], "row")]
    tms = [_row_tile(tp, 1024), bs * ts]
    tes = [_row_tile(tp, 512), bs * ts]

    w_gates = _gate_weights(nsa_w_in[0])
    nsa_w_out_bf16 = nsa_w_out.astype(BF16)
    mlstm_w_out_bf16 = mlstm_w_out.astype(BF16)
    g_, hd = NSA_KV_GROUPS, NSA_HEAD_DIM
    kvw2 = 2 * NSA_KV_WIDTH
    outs = {}
    for gi, (grp, tm, te) in enumerate(zip(groups, tms, tes)):
        b, t = grp.b, grp.t
        mod = grp.mod(0)
        h = _modulate(grp.x, norm_mix_g[0], mod, grp.kind, 0, 1, te, t)
        pos = jnp.arange(t) if gi == 0 else past + jnp.arange(t)
        cos, sin = _rope_tables(pos)
        if gi == 1:
            cos, sin = jnp.tile(cos, (b, 1)), jnp.tile(sin, (b, 1))
        q = _mm_rope(h, nsa_w_in, cos, sin, col0=0, n_out=NSA_Q_WIDTH, rope_cols=NSA_Q_WIDTH,
                     tm=tm, tn=512, out_dtype=BF16, name="nsa_q_proj").reshape(b, t, NSA_Q_WIDTH)
        kvp = lambda br, row_major: _kv_proj(h, nsa_w_in, cos, sin, col0=NSA_Q_WIDTH + br * kvw2, tm=te,
                                             row_major=row_major, name="nsa_kv_proj")
        cache_rows = lambda a: a.reshape(b, t * KV_SLOTS, hd)
        (kv_c_n,) = kvp(0, False)
        kv_s_n, kv_s = kvp(1, True)
        kv_w_n, kv_w = kvp(2, True)
        kv_c_n, kv_s_n, kv_w_n = cache_rows(kv_c_n), cache_rows(kv_s_n), cache_rows(kv_w_n)
        kv_s, kv_w = kv_s.reshape(b, t, kvw2), kv_w.reshape(b, t, kvw2)
        gates = _mm(h, w_gates, tm=tm, tn=g_ * LANES, act="sigmoid", name="nsa_gates").reshape(b, t, g_ * LANES)
        if gi == 0:
            ckv = _compress_tokens(kv_c_n, nsa_w_pool[0], 8 * NSA_BLOCK)
            att = _nsa_prompt_attention(q, ckv, kv_s, kv_w, gates, tq=128, kc=512)
            win_out = kv_w_n[:, (t - min(NSA_WINDOW, t)) * KV_SLOTS:]
        else:
            n_phys = cache_nsa_cmp_kv.shape[1]
            native = lambda a, lead: a.reshape(lead, -1, hd)
            ckv = _compress_pages(native(cache_nsa_cmp_kv, n_phys), page_table, nsa_w_pool[0],
                                  _row_tile(page_table.shape[1], 16))
            n_blocks = past // NSA_BLOCK + 1
            n_pad = -(-n_blocks // LANES) * LANES
            q4 = jnp.transpose(q.reshape(b, t, g_, NSA_REP, hd), (0, 2, 3, 1, 4)).reshape(b, g_, NSA_REP * t, hd)
            n_pg = 8
            o_c, bias = _nsa_sample_select(q4, ckv, t_new=t, past=past, n_blocks=n_blocks, n_pad=n_pad,
                                           n_keys=past + PAGE_SIZE, chunk=n_pg * PAGE_SIZE)
            att, win_out = _nsa_sample_attend(
                q4, bias, o_c, kv_s, native(state_nsa_win_kv, b), kv_w, gates,
                native(cache_nsa_sel_kv, n_phys), page_table, n_pg=n_pg, past=past)
        as_cache = lambda a: a.reshape(1, b, -1, 2, g_, hd)
        outs[gi] = (as_cache(kv_c_n), as_cache(kv_s_n), as_cache(win_out))
        x1, h2 = _mm_res_mod(att.reshape(b * t, NSA_Q_WIDTH), nsa_w_out_bf16, grp.x, mod, grp.kind, 2,
                             norm_ffn_g[0], 3, 4, tm=te, rows_per_batch=t)
        act = _swiglu_up(h2, ffn_w_gate, ffn_w_up, tm=tm, tn=512)
        grp.x = _mm_res(act, ffn_w_down, x1, mod, grp.kind, 5, tm=te, tn=512, rows_per_batch=t)

    nh, dk, dv = MLSTM_HEADS, MLSTM_QK_DIM, MLSTM_V_DIM
    n_main = 2 * nh * dk + 2 * nh * dv
    w_g = jnp.pad(mlstm_w_in[0][:, n_main:], ((0, 0), (0, LANES - 2 * nh)))
    states = {}
    mods = []
    n_tok = sum(grp.b * grp.t for grp in groups)
    h_all, row0 = None, 0
    for gi, (grp, tm, te) in enumerate(zip(groups, tms, tes)):
        b, t = grp.b, grp.t
        mod = grp.mod(1)
        h = _modulate(grp.x, norm_mix_g[1], mod, grp.kind, 0, 1, te, t)
        proj = _mm(h, mlstm_w_in, n_out=n_main, tm=tm, tn=1024, name="mlstm_in_proj").reshape(b, t, n_main)
        graw = _mm(h, w_g, tm=tm, tn=LANES, name="mlstm_gates")[:, :2 * nh].reshape(b, t, 2 * nh)
        if gi == 0:
            c0 = jnp.zeros((b, nh, dk, dv), F32)
            n0 = jnp.zeros((b, nh, dk), F32)
            m0 = jnp.zeros((b, nh), F32)
            chunk = MLSTM_CHUNK
        else:
            c0, n0, m0 = state_mlstm_c[0], state_mlstm_n[0], state_mlstm_m[0]
            chunk = t
        y, c_new, n_new, m_new = _mlstm(proj, graw, mlstm_b_gate[0], mlstm_norm_g[0], c0, n0, m0, chunk=chunk)
        states[gi] = (c_new[None], n_new.reshape(1, b, nh, dk), m_new.reshape(1, b, nh))
        grp.x, h_all = _mm_res_mod(y.reshape(b * t, nh * dv), mlstm_w_out_bf16, grp.x, mod, grp.kind, 2,
                                   norm_ffn_g[1], 3, 4, tm=te, rows_per_batch=t,
                                   into=h_all, row0=row0, total_rows=n_tok)
        row0 += b * t
        mods.append(mod)

    tm_moe = MOE_ROW_BLOCK
    wts, top_e = _router(h_all, moe_w_router[0], _row_tile(n_tok, 256))
    n_asg = n_tok * TOP_K
    flat_e = top_e.reshape(n_asg)
    onehot = (flat_e[:, None] == jnp.arange(N_EXPERTS)[None, :]).astype(jnp.int32)
    within = jnp.take_along_axis(jnp.cumsum(onehot, axis=0), flat_e[:, None], axis=1)[:, 0] - 1
    counts = jnp.sum(onehot, axis=0)
    padded = (counts + tm_moe - 1) // tm_moe * tm_moe
    p_end = jnp.cumsum(padded)
    dest = (p_end - padded)[flat_e] + within
    n_blk = -(-(n_asg + N_EXPERTS * (tm_moe - 1)) // tm_moe)
    n_rows = n_blk * tm_moe
    row_tok = jnp.zeros((n_rows,), jnp.int32).at[dest].set(jnp.arange(n_asg, dtype=jnp.int32) // TOP_K)
    blk_start = jnp.arange(n_blk, dtype=jnp.int32) * tm_moe
    blk_e = jnp.minimum(jnp.sum((p_end[None, :] <= blk_start[:, None]).astype(jnp.int32), axis=1), N_EXPERTS - 1)
    meta = jnp.concatenate([blk_e, p_end[-1:] // tm_moe]).astype(jnp.int32)
    xb = h_all[row_tok]
    act = _moe_up(xb, meta, moe_w_gate, moe_w_up, tm=tm_moe, tn=1024)
    yb = _moe_down(act, meta, moe_w_down, tm=tm_moe, tn=512)
    dest2 = dest.reshape(n_tok, TOP_K)

    finals = []
    off = 0
    for gi, (grp, te) in enumerate(zip(groups, tes)):
        m = grp.b * grp.t
        dg = dest2[off:off + m]
        fin = _combine_norm(grp.x, yb[dg[:, 0]], yb[dg[:, 1]], wts[off:off + m], mods[gi], grp.kind, 5,
                            norm_final_g, tm=te, rows_per_batch=grp.t)
        finals.append(fin.reshape(grp.b, grp.t, d))
        off += m

    return (finals[0], finals[1],
            outs[0][0], outs[0][1], outs[0][2], states[0][0], states[0][1], states[0][2],
            outs[1][0], outs[1][1], outs[1][2], states[1][0], states[1][1], states[1][2])
```

```python
import functools

import jax
import jax.numpy as jnp
from jax import lax
from jax.experimental import pallas as pl
from jax.experimental.pallas import tpu as pltpu

F32 = jnp.float32
BF16 = jnp.bfloat16

PAGE_SIZE = 128
NSA_HEADS = 16
NSA_HEAD_DIM = 128
NSA_KV_GROUPS = 4
NSA_REP = NSA_HEADS // NSA_KV_GROUPS
NSA_BLOCK = 64
NSA_TOP_BLOCKS = 16
NSA_WINDOW = 512
NSA_Q_WIDTH = NSA_HEADS * NSA_HEAD_DIM
NSA_KV_WIDTH = NSA_KV_GROUPS * NSA_HEAD_DIM
ROPE_THETA = 10000.0
FORCE_SCORE = 1.0e4
NEG_INF = -1.0e30
MLSTM_HEADS = 8
MLSTM_QK_DIM = 128
MLSTM_V_DIM = 256
N_EXPERTS = 8
TOP_K = 2
MOE_ROW_BLOCK = 512
NORM_EPS = 1e-6
LOG2_E = 1.4426950408889634

LANES = 128
SUBLANES = 8
VMEM_LIMIT_BYTES = 52 * 1024 * 1024

MM_ROW_TILE = 1024
ROW_TILE = 512
MM_COL_TILE = 512
MM_WIDE_COL_TILE = 1024
ATTN_Q_TILE = 256
ATTN_KEY_CHUNK = 512
COMPRESS_TOKENS = 512
PAGES_PER_STEP = 16
MLSTM_STEP_TOKENS = 256

_NT = (((1,), (1,)), ((), ()))
_TN = (((0,), (0,)), ((), ()))


def _params(*sem):
    return pltpu.CompilerParams(dimension_semantics=sem, vmem_limit_bytes=VMEM_LIMIT_BYTES)


def _dot(a, b):
    return jnp.dot(a.astype(BF16), b.astype(BF16), preferred_element_type=F32)


def _dot_nt(a, b):
    return lax.dot_general(a.astype(BF16), b.astype(BF16), _NT, preferred_element_type=F32)


def _sigmoid(x):
    return 1.0 / (1.0 + jnp.exp(-x))


def _silu(x):
    return x * _sigmoid(x)


def _adaln_kernel(c_ref, w_ref, b_ref, o_ref):
    o_ref[...] = _dot(_silu(c_ref[...]), w_ref[...]) + b_ref[...]


def _adaln(c_all, ada_w, ada_b):
    depth, d, n = ada_w.shape
    tn = _row_tile(n, MM_WIDE_COL_TILE)
    rows = c_all.shape[0]
    return pl.pallas_call(
        _adaln_kernel,
        grid=(depth, n // tn),
        in_specs=[pl.BlockSpec((rows, d), lambda l, j: (0, 0)),
                  pl.BlockSpec((None, d, tn), lambda l, j: (l, 0, j)),
                  pl.BlockSpec((None, 1, tn), lambda l, j: (l, 0, j))],
        out_specs=pl.BlockSpec((None, rows, tn), lambda l, j: (l, 0, j)),
        out_shape=jax.ShapeDtypeStruct((depth, rows, n), F32),
        compiler_params=_params("arbitrary", "arbitrary"),
        name="adaln",
    )(c_all, ada_w, ada_b.reshape(depth, 1, n))


def _rms(x):
    return x * lax.rsqrt(jnp.mean(x * x, axis=-1, keepdims=True) + NORM_EPS)


def _modulate_kernel(x_ref, g_ref, shift_ref, scale_ref, o_ref):
    y = _rms(x_ref[...]) * g_ref[...]
    o_ref[...] = (y * (1.0 + scale_ref[...]) + shift_ref[...]).astype(o_ref.dtype)


def _mod_spec(mod, kind, chunk, width, tm, rows_per_batch):
    d6 = mod.shape[-1]
    per_chunk = (d6 // 6) // width
    if kind == "batch":
        return pl.BlockSpec((None, 1, width),
                            lambda i, j: ((i * tm) // rows_per_batch, 0, chunk * per_chunk + j))
    return pl.BlockSpec((tm, width), lambda i, j: (i, chunk * per_chunk + j))


def _modulate(x, g, mod, kind, shift_chunk, scale_chunk, tm, rows_per_batch):
    m, d = x.shape
    return pl.pallas_call(
        _modulate_kernel,
        grid=(m // tm, 1),
        in_specs=[pl.BlockSpec((tm, d), lambda i, j: (i, 0)),
                  pl.BlockSpec((1, d), lambda i, j: (0, 0)),
                  _mod_spec(mod, kind, shift_chunk, d, tm, rows_per_batch),
                  _mod_spec(mod, kind, scale_chunk, d, tm, rows_per_batch)],
        out_specs=pl.BlockSpec((tm, d), lambda i, j: (i, 0)),
        out_shape=jax.ShapeDtypeStruct((m, d), BF16),
        compiler_params=_params("arbitrary", "arbitrary"),
        name="modulate",
    )(x, g.reshape(1, d), mod, mod)


def _mm_res_mod_kernel(*refs, aliased, router):
    a_ref, w_ref, res_ref, gate_ref, g_ref, shift_ref, scale_ref = refs[:7]
    rest = list(refs[7:])
    wr_ref = rest.pop(0) if router else None
    if aliased:
        rest.pop(0)
    x_ref, h_ref = rest[0], rest[1]
    x = res_ref[...] + gate_ref[...] * _dot(a_ref[...], w_ref[...])
    x_ref[...] = x
    h = (_rms(x) * g_ref[...] * (1.0 + scale_ref[...]) + shift_ref[...]).astype(h_ref.dtype)
    h_ref[...] = h
    if router:
        rest[2][...] = _route(_dot(h, wr_ref[...]))


def _mm_res_mod(a, w, res, mod, kind, gate_chunk, g, shift_chunk, scale_chunk, *, tm, rows_per_batch,
                into=None, row0=0, total_rows=None, w_router=None):
    m, k = a.shape
    n = w.shape[-1]
    h_rows = m if total_rows is None else total_rows
    blk0 = row0 // tm
    assert row0 % tm == 0
    row = pl.BlockSpec((tm, n), lambda i, j: (i, 0))
    mspec = lambda chunk: _mod_spec(mod, kind, chunk, n, tm, rows_per_batch)
    in_specs = [pl.BlockSpec((tm, k), lambda i, j: (i, 0)),
                _w_spec(w, n, lambda i, j: 0),
                row, mspec(gate_chunk),
                pl.BlockSpec((1, n), lambda i, j: (0, 0)), mspec(shift_chunk), mspec(scale_chunk)]
    args = [a, w, res, mod, g.reshape(1, n), mod, mod]
    out_specs = [row, pl.BlockSpec((tm, n), lambda i, j: (blk0 + i, 0))]
    out_shape = [jax.ShapeDtypeStruct((m, n), F32), jax.ShapeDtypeStruct((h_rows, n), BF16)]
    if w_router is not None:
        in_specs.append(pl.BlockSpec(w_router.shape, lambda i, j: (0, 0)))
        args.append(w_router)
        out_specs.append(pl.BlockSpec((tm, LANES), lambda i, j: (i, 0)))
        out_shape.append(jax.ShapeDtypeStruct((m, LANES), F32))
    aliases = {}
    if into is not None:
        in_specs.append(pl.BlockSpec(memory_space=pl.ANY))
        args.append(into)
        aliases = {len(args) - 1: 1}
    return pl.pallas_call(
        functools.partial(_mm_res_mod_kernel, aliased=into is not None, router=w_router is not None),
        grid=(m // tm, 1),
        in_specs=in_specs,
        out_specs=out_specs,
        out_shape=out_shape,
        input_output_aliases=aliases,
        compiler_params=_params("arbitrary", "arbitrary"),
        name="mm_residual_modulate",
    )(*args)


def _mm_kernel(x_ref, w_ref, o_ref, *, act):
    acc = _dot(x_ref[...], w_ref[...])
    if act == "sigmoid":
        acc = _sigmoid(acc)
    o_ref[...] = acc.astype(o_ref.dtype)


def _w_spec(w, tn, col_of):
    lead = w.ndim - 2
    assert all(s == 1 for s in w.shape[:lead])
    return pl.BlockSpec((None,) * lead + (w.shape[-2], tn), lambda *idx: (0,) * lead + (0, col_of(*idx)))


def _mm(x, w, *, col0=0, n_out=None, tm, tn, act=None, out_dtype=F32, name="mm"):
    m, k = x.shape
    n = w.shape[-1] if n_out is None else n_out
    return pl.pallas_call(
        functools.partial(_mm_kernel, act=act),
        grid=(n // tn, m // tm),
        in_specs=[pl.BlockSpec((tm, k), lambda j, i: (i, 0)),
                  _w_spec(w, tn, lambda j, i: col0 // tn + j)],
        out_specs=pl.BlockSpec((tm, tn), lambda j, i: (i, j)),
        out_shape=jax.ShapeDtypeStruct((m, n), out_dtype),
        compiler_params=_params("arbitrary", "arbitrary"),
        name=name,
    )(x, w)


def _mm_rope_kernel(x_ref, w_ref, cos_ref, sin_ref, o_ref, *, tn, rope_tiles):
    acc = _dot(x_ref[...], w_ref[...])

    @pl.when(pl.program_id(0) < rope_tiles)
    def _():
        cos = cos_ref[...]
        sin = sin_ref[...]
        for s in range(tn // NSA_HEAD_DIM):
            sl = slice(s * NSA_HEAD_DIM, (s + 1) * NSA_HEAD_DIM)
            a = acc[:, sl]
            o_ref[:, sl] = (a * cos + pltpu.roll(a, NSA_HEAD_DIM // 2, 1) * sin).astype(o_ref.dtype)

    @pl.when(pl.program_id(0) >= rope_tiles)
    def _():
        o_ref[...] = acc.astype(o_ref.dtype)


def _mm_rope(x, w, cos, sin, *, col0, n_out, rope_cols, tm, tn, out_dtype, name):
    m, k = x.shape
    pos_tiles = cos.shape[0] // tm
    return pl.pallas_call(
        functools.partial(_mm_rope_kernel, tn=tn, rope_tiles=rope_cols // tn),
        grid=(n_out // tn, m // tm),
        in_specs=[pl.BlockSpec((tm, k), lambda j, i: (i, 0)),
                  _w_spec(w, tn, lambda j, i: col0 // tn + j),
                  pl.BlockSpec((tm, NSA_HEAD_DIM), lambda j, i: (i % pos_tiles, 0)),
                  pl.BlockSpec((tm, NSA_HEAD_DIM), lambda j, i: (i % pos_tiles, 0))],
        out_specs=pl.BlockSpec((tm, tn), lambda j, i: (i, j)),
        out_shape=jax.ShapeDtypeStruct((m, n_out), out_dtype),
        compiler_params=_params("arbitrary", "arbitrary"),
        name=name,
    )(x, w, cos, sin)


def _kv_proj_kernel(x_ref, w_ref, cos_ref, sin_ref, *o_refs, row_major):
    acc = _dot(x_ref[...], w_ref[...])
    cos = cos_ref[...]
    sin = sin_ref[...]
    hd = NSA_HEAD_DIM
    tm = acc.shape[0]
    for slot in range(KV_SLOTS):
        a = acc[:, slot * hd:(slot + 1) * hd]
        if slot < NSA_KV_GROUPS:
            a = a * cos + pltpu.roll(a, hd // 2, 1) * sin
        o_refs[0][pl.ds(slot, tm, stride=KV_SLOTS), :] = a
        if row_major:
            o_refs[1][:, slot * hd:(slot + 1) * hd] = a


def _kv_proj(x, w, cos, sin, *, col0, tm, row_major, name):
    m, k = x.shape
    hd = NSA_HEAD_DIM
    kvw2 = KV_SLOTS * hd
    pos_tiles = cos.shape[0] // tm
    out_specs = [pl.BlockSpec((tm * KV_SLOTS, hd), lambda i: (i, 0))]
    out_shape = [jax.ShapeDtypeStruct((m * KV_SLOTS, hd), F32)]
    if row_major:
        out_specs.append(pl.BlockSpec((tm, kvw2), lambda i: (i, 0)))
        out_shape.append(jax.ShapeDtypeStruct((m, kvw2), F32))
    return pl.pallas_call(
        functools.partial(_kv_proj_kernel, row_major=row_major),
        grid=(m // tm,),
        in_specs=[pl.BlockSpec((tm, k), lambda i: (i, 0)),
                  _w_spec(w, kvw2, lambda i: col0 // kvw2),
                  pl.BlockSpec((tm, hd), lambda i: (i % pos_tiles, 0)),
                  pl.BlockSpec((tm, hd), lambda i: (i % pos_tiles, 0))],
        out_specs=out_specs,
        out_shape=out_shape,
        compiler_params=_params("arbitrary"),
        name=name,
    )(x, w, cos, sin)


def _mm_res_kernel(a_ref, w_ref, res_ref, gate_ref, o_ref):
    o_ref[...] = res_ref[...] + gate_ref[...] * _dot(a_ref[...], w_ref[...])


def _mm_res(a, w, res, mod, kind, gate_chunk, *, tm, tn, rows_per_batch):
    m, k = a.shape
    n = w.shape[-1]
    tn = _row_tile(n, tn)
    swap = lambda f: (lambda j, i: f(i, j))
    gspec = _mod_spec(mod, kind, gate_chunk, tn, tm, rows_per_batch)
    gspec = pl.BlockSpec(gspec.block_shape, swap(gspec.index_map))
    return pl.pallas_call(
        _mm_res_kernel,
        grid=(n // tn, m // tm),
        in_specs=[pl.BlockSpec((tm, k), lambda j, i: (i, 0)),
                  _w_spec(w, tn, lambda j, i: j),
                  pl.BlockSpec((tm, tn), lambda j, i: (i, j)),
                  gspec],
        out_specs=pl.BlockSpec((tm, tn), lambda j, i: (i, j)),
        out_shape=jax.ShapeDtypeStruct((m, n), F32),
        compiler_params=_params("arbitrary", "arbitrary"),
        name="mm_residual",
    )(a, w, res, mod)


def _swiglu_up_kernel(x_ref, wg_ref, wu_ref, o_ref):
    x = x_ref[...]
    o_ref[...] = (_silu(_dot(x, wg_ref[...])) * _dot(x, wu_ref[...])).astype(o_ref.dtype)


def _swiglu_up(x, wg, wu, *, tm, tn):
    m, k = x.shape
    f = wg.shape[-1]
    return pl.pallas_call(
        _swiglu_up_kernel,
        grid=(pl.cdiv(f, tn), m // tm),
        in_specs=[pl.BlockSpec((tm, k), lambda j, i: (i, 0)),
                  _w_spec(wg, tn, lambda j, i: j),
                  _w_spec(wu, tn, lambda j, i: j)],
        out_specs=pl.BlockSpec((tm, tn), lambda j, i: (i, j)),
        out_shape=jax.ShapeDtypeStruct((m, f), BF16),
        compiler_params=_params("arbitrary", "arbitrary"),
        name="swiglu_up",
    )(x, wg, wu)


def _grouped_kernel(meta_ref, *refs, body, n_blk):
    @pl.when(pl.program_id(1) < meta_ref[n_blk])
    def _():
        body(*refs)


def _grouped_specs(n_blk, tm, k_in, tn, weights):
    row = lambda r, meta: jnp.minimum(r, meta[n_blk] - 1)
    x_spec = pl.BlockSpec((tm, k_in), lambda j, r, meta: (row(r, meta), 0))
    w_specs = [pl.BlockSpec((None, None, w.shape[-2], tn), lambda j, r, meta: (0, meta[row(r, meta)], 0, j))
               for w in weights]
    o_spec = pl.BlockSpec((tm, tn), lambda j, r, meta: (row(r, meta), j))
    return [x_spec] + w_specs, o_spec


def _moe_up(xb, meta, wg, wu, *, tm, tn):
    rows, k = xb.shape
    f = wg.shape[-1]
    tn = _row_tile(f, tn)
    n_blk = rows // tm
    in_specs, o_spec = _grouped_specs(n_blk, tm, k, tn, [wg, wu])
    grid_spec = pltpu.PrefetchScalarGridSpec(
        num_scalar_prefetch=1, grid=(f // tn, n_blk), in_specs=in_specs, out_specs=o_spec)
    return pl.pallas_call(
        functools.partial(_grouped_kernel, body=_swiglu_up_kernel, n_blk=n_blk),
        grid_spec=grid_spec,
        out_shape=jax.ShapeDtypeStruct((rows, f), BF16),
        compiler_params=_params("arbitrary", "arbitrary"),
        name="moe_up",
    )(meta, xb, wg, wu)


def _mm_plain_kernel(a_ref, w_ref, o_ref):
    o_ref[...] = _dot(a_ref[...], w_ref[...])


def _moe_down(act, meta, wd, *, tm, tn):
    rows, f = act.shape
    d = wd.shape[-1]
    tn = _row_tile(d, tn)
    n_blk = rows // tm
    in_specs, o_spec = _grouped_specs(n_blk, tm, f, tn, [wd])
    grid_spec = pltpu.PrefetchScalarGridSpec(
        num_scalar_prefetch=1, grid=(d // tn, n_blk), in_specs=in_specs, out_specs=o_spec)
    return pl.pallas_call(
        functools.partial(_grouped_kernel, body=_mm_plain_kernel, n_blk=n_blk),
        grid_spec=grid_spec,
        out_shape=jax.ShapeDtypeStruct((rows, d), F32),
        compiler_params=_params("arbitrary", "arbitrary"),
        name="moe_down",
    )(meta, act, wd)


def _route(lg):
    lane = lax.broadcasted_iota(jnp.int32, lg.shape, 1)
    x = jnp.where(lane < N_EXPERTS, lg, -jnp.inf)
    v0 = jnp.max(x, axis=-1, keepdims=True)
    i0 = jnp.min(jnp.where(x == v0, lane, LANES), axis=-1, keepdims=True)
    x1 = jnp.where(lane == i0, -jnp.inf, x)
    v1 = jnp.max(x1, axis=-1, keepdims=True)
    i1 = jnp.min(jnp.where(x1 == v1, lane, LANES), axis=-1, keepdims=True)
    e1 = jnp.exp(v1 - v0)
    g0 = 1.0 / (1.0 + e1)
    g1 = e1 / (1.0 + e1)
    return jnp.where(lane == 0, g0,
                     jnp.where(lane == 1, g1,
                               jnp.where(lane == 2, i0.astype(F32),
                                         jnp.where(lane == 3, i1.astype(F32), 0.0))))


def _combine_norm_kernel(x_ref, y0_ref, y1_ref, w_ref, gate_ref, g_ref, o_ref):
    w = w_ref[...]
    y = w[:, 0:1] * y0_ref[...] + w[:, 1:2] * y1_ref[...]
    xo = x_ref[...] + gate_ref[...] * y
    o_ref[...] = _rms(xo) * g_ref[...]


def _combine_norm(x, y0, y1, wts, mod, kind, gate_chunk, g, *, tm, rows_per_batch):
    m, d = x.shape
    row = pl.BlockSpec((tm, d), lambda i, j: (i, 0))
    return pl.pallas_call(
        _combine_norm_kernel,
        grid=(m // tm, 1),
        in_specs=[row, row, row,
                  pl.BlockSpec((tm, TOP_K), lambda i, j: (i, 0)),
                  _mod_spec(mod, kind, gate_chunk, d, tm, rows_per_batch),
                  pl.BlockSpec((1, d), lambda i, j: (0, 0))],
        out_specs=row,
        out_shape=jax.ShapeDtypeStruct((m, d), F32),
        compiler_params=_params("arbitrary", "arbitrary"),
        name="moe_combine_final_norm",
    )(x, y0, y1, wts, mod, g.reshape(1, d))


KV_SLOTS = 2 * NSA_KV_GROUPS


def _slot_rows(ref, slot, n_tok, first_tok=0):
    return ref[pl.ds(first_tok * KV_SLOTS + slot, n_tok, stride=KV_SLOTS), :]


def _compress_native_kernel(*refs, n_in):
    w_ref, o_ref = refs[n_in], refs[n_in + 1]
    hd = NSA_HEAD_DIM
    row = 0
    for x_ref in refs[:n_in]:
        n_tok = x_ref.shape[0] // KV_SLOTS
        nb = n_tok // NSA_BLOCK
        for slot in range(KV_SLOTS):
            x = _slot_rows(x_ref, slot, n_tok)
            w = w_ref[slot // NSA_KV_GROUPS]
            pooled = jnp.sum(x.reshape(nb, NSA_BLOCK, hd) * w[None], axis=1)
            o_ref[row:row + nb, slot * hd:(slot + 1) * hd] = pooled
        row += nb


def _compress_pages_kernel(pt_ref, *refs, n_in):
    del pt_ref
    _compress_native_kernel(*refs, n_in=n_in)


def _compress_tokens(kv, w_pool, tokens_per_step):
    b, rows, hd = kv.shape
    t = rows // KV_SLOTS
    nb = tokens_per_step // NSA_BLOCK
    return pl.pallas_call(
        functools.partial(_compress_native_kernel, n_in=1),
        grid=(b, t // tokens_per_step),
        in_specs=[pl.BlockSpec((None, tokens_per_step * KV_SLOTS, hd), lambda i, s: (i, s, 0)),
                  pl.BlockSpec(w_pool.shape, lambda i, s: (0, 0, 0))],
        out_specs=pl.BlockSpec((None, nb, KV_SLOTS * hd), lambda i, s: (i, s, 0)),
        out_shape=jax.ShapeDtypeStruct((b, t // NSA_BLOCK, KV_SLOTS * hd), F32),
        compiler_params=_params("arbitrary", "arbitrary"),
        name="nsa_compress_prompt",
    )(kv, w_pool)


def _compress_pages(cache, page_table, w_pool, pages_per_step):
    bsz, n_pages = page_table.shape
    rows, hd = cache.shape[1:]
    p = pages_per_step
    bpp = PAGE_SIZE // NSA_BLOCK
    kvw = KV_SLOTS * hd

    def page_spec(k):
        return pl.BlockSpec((None, rows, hd), lambda i, s, pt: (pt[i, s * p + k], 0, 0))

    grid_spec = pltpu.PrefetchScalarGridSpec(
        num_scalar_prefetch=1,
        grid=(bsz, n_pages // p),
        in_specs=[page_spec(k) for k in range(p)] + [pl.BlockSpec(w_pool.shape, lambda i, s, pt: (0, 0, 0))],
        out_specs=pl.BlockSpec((None, p * bpp, kvw), lambda i, s, pt: (i, s, 0)))
    return pl.pallas_call(
        functools.partial(_compress_pages_kernel, n_in=p),
        grid_spec=grid_spec,
        out_shape=jax.ShapeDtypeStruct((bsz, n_pages * bpp, kvw), F32),
        compiler_params=_params("arbitrary", "arbitrary"),
        name="nsa_compress_pages",
    )(page_table, *([cache] * p), w_pool)


def _select_blocks(score, n_valid):
    rows, width = score.shape
    col = lax.broadcasted_iota(jnp.int32, (rows, width), 1)
    rank = jnp.zeros((rows, width), F32)
    for m in range(n_valid):
        c = score[:, m:m + 1]
        beats = jnp.where(c > score, 1.0, jnp.where((c == score) & (col > m), 1.0, 0.0))
        rank = rank + beats
    return jnp.where((rank < NSA_TOP_BLOCKS) & (col < n_valid), 1.0, 0.0)


def _softmax_rows(s):
    e = jnp.exp(s - jnp.max(s, axis=-1, keepdims=True))
    return e, jnp.sum(e, axis=-1, keepdims=True)


def _compressed_branch(q4, ck, cv, tpos, reps):
    scale = NSA_HEAD_DIM ** -0.5
    nc = ck.shape[0]
    tq = q4.shape[0] // reps
    s = _dot_nt(q4, ck) * scale
    blk = lax.broadcasted_iota(jnp.int32, (1, nc), 1)
    ok = ((blk + 1) * NSA_BLOCK - 1) <= tpos
    e, l = _softmax_rows(jnp.where(ok, s, NEG_INF))
    p = jnp.where(ok, e / l, 0.0)
    o_c = _dot(p, cv)
    imp = p[0:tq]
    for r in range(1, reps):
        imp = imp + p[r * tq:(r + 1) * tq]
    return o_c, imp


def _compressed_select_t(q4, ck, cv, qpos_row, reps):
    scale = NSA_HEAD_DIM ** -0.5
    nc = ck.shape[0]
    tq = q4.shape[0] // reps
    tpos = jnp.concatenate([qpos_row] * reps, axis=1)
    blk = lax.broadcasted_iota(jnp.int32, (nc, 1), 0)
    s = _dot_nt(ck, q4) * scale
    ok = ((blk + 1) * NSA_BLOCK - 1) <= tpos
    s = jnp.where(ok, s, NEG_INF)
    e = jnp.exp(s - jnp.max(s, axis=0, keepdims=True))
    p = jnp.where(ok, e / jnp.sum(e, axis=0, keepdims=True), 0.0)
    o_c = lax.dot_general(p.astype(BF16), cv.astype(BF16), _TN, preferred_element_type=F32)
    imp = p[:, 0:tq]
    for r in range(1, reps):
        imp = imp + p[:, r * tq:(r + 1) * tq]
    cur = qpos_row // NSA_BLOCK
    forced = (blk == 0) | (blk == cur) | (blk == cur - 1)
    score = jnp.where(forced, FORCE_SCORE, jnp.where(blk <= cur, imp, -1.0))
    row = lax.broadcasted_iota(jnp.int32, (nc, tq), 0)
    rank = jnp.zeros((nc, tq), F32)
    for m in range(nc):
        c = score[m:m + 1, :]
        rank = rank + jnp.where(c > score, 1.0, jnp.where((c == score) & (row > m), 1.0, 0.0))
    return o_c, jnp.where(rank < NSA_TOP_BLOCKS, 1.0, 0.0)


def _block_scores(imp, qpos, n_blocks_pad):
    tq, n_imp = imp.shape
    if n_blocks_pad > n_imp:
        imp = jnp.concatenate([imp, jnp.zeros((tq, n_blocks_pad - n_imp), F32)], axis=1)
    blk = lax.broadcasted_iota(jnp.int32, (1, n_blocks_pad), 1)
    cur = qpos // NSA_BLOCK
    forced = (blk == 0) | (blk == cur) | (blk == cur - 1)
    return jnp.where(forced, FORCE_SCORE, jnp.where(blk <= cur, imp, -1.0))


def _nsa_prompt_kernel(q_ref, ck_ref, cv_ref, ks_ref, vs_ref, kw_ref, vw_ref, gate_ref, o_ref, *, tq, kc, seq):
    hd, reps = NSA_HEAD_DIM, NSA_REP
    c2 = (hd ** -0.5) * LOG2_E
    q0 = pl.program_id(2) * tq
    qf = q_ref[...]
    q4 = jnp.concatenate([qf[:, r * hd:(r + 1) * hd] for r in range(reps)], axis=0).astype(BF16)
    qpos = q0 + lax.broadcasted_iota(jnp.int32, (tq, 1), 0)
    m_rows = reps * tq
    stack = lambda x: jnp.concatenate([x] * reps, axis=0)

    n_blocks = seq // NSA_BLOCK
    qpos_row = q0 + lax.broadcasted_iota(jnp.int32, (1, tq), 1)
    o_c, sel = _compressed_select_t(q4, ck_ref[...], cv_ref[...], qpos_row, reps)
    sel = sel.astype(BF16)

    blk_row = lax.broadcasted_iota(jnp.int32, (n_blocks, 1), 0)

    def chunk(c, carry):
        m_i, l_i, acc = carry
        k0 = pl.multiple_of(c * kc, kc)
        kk = ks_ref[pl.ds(k0, kc), :]
        vv = vs_ref[pl.ds(k0, kc), :]
        kpos = k0 + lax.broadcasted_iota(jnp.int32, (1, kc), 1)
        expand = jnp.where(kpos // NSA_BLOCK == blk_row, 1.0, 0.0).astype(BF16)
        selk = lax.dot_general(sel, expand, _TN, preferred_element_type=F32)
        bias = jnp.where(kpos <= qpos, (1.0 - selk) * NEG_INF, NEG_INF)
        s = _dot_nt(q4, kk) * c2 + stack(bias)
        m_new = jnp.maximum(m_i, jnp.max(s, axis=-1, keepdims=True))
        alpha = jnp.exp2(m_i - m_new)
        p = jnp.exp2(s - m_new)
        l_new = alpha * l_i + jnp.sum(p, axis=-1, keepdims=True)
        return m_new, l_new, alpha * acc + _dot(p, vv)

    n_chunks = (q0 + tq + kc - 1) // kc
    init = (jnp.full((m_rows, 1), NEG_INF, F32), jnp.zeros((m_rows, 1), F32), jnp.zeros((m_rows, hd), F32))
    _, l_s, acc_s = lax.fori_loop(0, n_chunks, chunk, init)
    o_s = acc_s / l_s

    span = NSA_WINDOW + tq
    w0 = pl.multiple_of(jnp.maximum(q0 - NSA_WINDOW, 0), tq)
    dist = qpos - (w0 + lax.broadcasted_iota(jnp.int32, (1, span), 1))
    bias = jnp.where(dist >= 0, jnp.where(dist < NSA_WINDOW, 0.0, NEG_INF), NEG_INF)
    s = _dot_nt(q4, kw_ref[pl.ds(w0, span), :]) * c2 + stack(bias)
    e = jnp.exp2(s - jnp.max(s, axis=-1, keepdims=True))
    o_w = _dot(e, vw_ref[pl.ds(w0, span), :]) / jnp.sum(e, axis=-1, keepdims=True)

    gate = gate_ref[...]
    for r in range(reps):
        rows = slice(r * tq, (r + 1) * tq)
        o = (gate[:, 3 * r:3 * r + 1] * o_c[rows] + gate[:, 3 * r + 1:3 * r + 2] * o_s[rows]
             + gate[:, 3 * r + 2:3 * r + 3] * o_w[rows])
        o_ref[:, r * hd:(r + 1) * hd] = o.astype(o_ref.dtype)


def _nsa_prompt_attention(q, ckv, kv_s, kv_w, gates, *, tq, kc):
    b, t, _ = q.shape
    g_, hd = NSA_KV_GROUPS, NSA_HEAD_DIM
    qw = NSA_REP * hd
    nc = ckv.shape[1]
    keys = pl.BlockSpec((None, t, hd), lambda i, g, s: (i, 0, g))
    vals = pl.BlockSpec((None, t, hd), lambda i, g, s: (i, 0, g_ + g))
    return pl.pallas_call(
        functools.partial(_nsa_prompt_kernel, tq=tq, kc=kc, seq=t),
        grid=(b, g_, t // tq),
        in_specs=[pl.BlockSpec((None, tq, qw), lambda i, g, s: (i, s, g)),
                  pl.BlockSpec((None, nc, hd), lambda i, g, s: (i, 0, g)),
                  pl.BlockSpec((None, nc, hd), lambda i, g, s: (i, 0, g_ + g)),
                  keys, vals, keys, vals,
                  pl.BlockSpec((None, tq, LANES), lambda i, g, s: (i, s, g))],
        out_specs=pl.BlockSpec((None, tq, qw), lambda i, g, s: (i, s, g)),
        out_shape=jax.ShapeDtypeStruct((b, t, NSA_Q_WIDTH), BF16),
        compiler_params=_params("arbitrary", "arbitrary", "arbitrary"),
        name="nsa_prompt_attention",
    )(q, ckv, ckv, kv_s, kv_s, kv_w, kv_w, gates)


def _nsa_sample_select_kernel(q_ref, ckv_ref, oc_ref, bias_ref, *, t_new, past, n_blocks, n_pad, chunk):
    hd, reps = NSA_HEAD_DIM, NSA_REP
    qpos = past + lax.broadcasted_iota(jnp.int32, (t_new, 1), 0)
    tpos = jnp.concatenate([qpos] * reps, axis=0)
    sels = []
    for g in range(NSA_KV_GROUPS):
        ck = ckv_ref[:, g * hd:(g + 1) * hd]
        cv = ckv_ref[:, NSA_KV_WIDTH + g * hd:NSA_KV_WIDTH + (g + 1) * hd]
        o_c, imp = _compressed_branch(q_ref[g].astype(BF16), ck, cv, tpos, reps)
        oc_ref[g] = o_c
        sels.append(_select_blocks(_block_scores(imp, qpos, n_pad), n_blocks).astype(BF16))
    n_keys = bias_ref.shape[-1]
    blk_row = lax.broadcasted_iota(jnp.int32, (n_pad, 1), 0)
    for k0 in range(0, n_keys, chunk):
        width = min(chunk, n_keys - k0)
        kpos = k0 + lax.broadcasted_iota(jnp.int32, (1, width), 1)
        expand = jnp.where(kpos // NSA_BLOCK == blk_row, 1.0, 0.0).astype(BF16)
        for g in range(NSA_KV_GROUPS):
            selk = jnp.dot(sels[g], expand, preferred_element_type=F32)
            bias_ref[g, :, k0:k0 + width] = jnp.where(kpos <= qpos, (1.0 - selk) * NEG_INF, NEG_INF)


def _nsa_sample_select(q4, ckv, *, t_new, past, n_blocks, n_pad, n_keys, chunk):
    b = q4.shape[0]
    g_, hd = NSA_KV_GROUPS, NSA_HEAD_DIM
    rows = NSA_REP * t_new
    nc = ckv.shape[1]
    return pl.pallas_call(
        functools.partial(_nsa_sample_select_kernel, t_new=t_new, past=past, n_blocks=n_blocks, n_pad=n_pad,
                          chunk=chunk),
        grid=(b,),
        in_specs=[pl.BlockSpec((None, g_, rows, hd), lambda i: (i, 0, 0, 0)),
                  pl.BlockSpec((None, nc, 2 * NSA_KV_WIDTH), lambda i: (i, 0, 0))],
        out_specs=[pl.BlockSpec((None, g_, rows, hd), lambda i: (i, 0, 0, 0)),
                   pl.BlockSpec((None, g_, t_new, n_keys), lambda i: (i, 0, 0, 0))],
        out_shape=[jax.ShapeDtypeStruct((b, g_, rows, hd), F32),
                   jax.ShapeDtypeStruct((b, g_, t_new, n_keys), F32)],
        compiler_params=_params("arbitrary"),
        name="nsa_sample_select",
    )(q4, ckv)


def _nsa_sample_attend_kernel(pt_ref, *refs, n_pg, t_new, past):
    del pt_ref
    pages = refs[:n_pg]
    (q_ref, bias_ref, bias_new_ref, oc_ref, new_s_ref, win_ref, new_w_ref, gate_ref,
     o_ref, win_out_ref, m_ref, l_ref, acc_ref) = refs[n_pg:]
    hd, reps, g_ = NSA_HEAD_DIM, NSA_REP, NSA_KV_GROUPS
    kvw = NSA_KV_WIDTH
    scale = hd ** -0.5
    c2 = scale * LOG2_E
    step = pl.program_id(1)
    qpos = past + lax.broadcasted_iota(jnp.int32, (t_new, 1), 0)
    tpos = jnp.concatenate([qpos] * reps, axis=0)
    stack = lambda x: jnp.concatenate([x] * reps, axis=0)

    def online(keys, vals, biases):
        groups = range(g_)
        m_i = [m_ref[g] for g in groups]
        l_i = [l_ref[g] for g in groups]
        acc_i = [acc_ref[g] for g in groups]
        s = [_dot_nt(q_ref[g], keys[g]) * c2 + stack(biases[g]) for g in groups]
        m_new = [jnp.maximum(m_i[g], jnp.max(s[g], axis=-1, keepdims=True)) for g in groups]
        alpha = [jnp.exp2(m_i[g] - m_new[g]) for g in groups]
        p = [jnp.exp2(s[g] - m_new[g]) for g in groups]
        pv = [_dot(p[g], vals[g]) for g in groups]
        for g in groups:
            l_ref[g] = alpha[g] * l_i[g] + jnp.sum(p[g], axis=-1, keepdims=True)
            acc_ref[g] = alpha[g] * acc_i[g] + pv[g]
            m_ref[g] = m_new[g]

    def new_rows(ref, c0):
        x = ref[:, c0:c0 + hd]
        return jnp.concatenate([x, jnp.zeros((PAGE_SIZE - t_new, hd), x.dtype)], axis=0)

    @pl.when(step == 0)
    def _():
        m_ref[...] = jnp.full(m_ref.shape, NEG_INF, F32)
        l_ref[...] = jnp.zeros(l_ref.shape, F32)
        acc_ref[...] = jnp.zeros(acc_ref.shape, F32)
        online([new_rows(new_s_ref, g * hd) for g in range(g_)],
               [new_rows(new_s_ref, kvw + g * hd) for g in range(g_)],
               [bias_new_ref[g] for g in range(g_)])

    page_rows = lambda slot: jnp.concatenate([_slot_rows(pg, slot, PAGE_SIZE).astype(BF16) for pg in pages], axis=0)
    online([page_rows(g) for g in range(g_)], [page_rows(g_ + g) for g in range(g_)],
           [bias_ref[g] for g in range(g_)])

    @pl.when(step == pl.num_programs(1) - 1)
    def _():
        wb = win_ref.shape[0] // KV_SLOTS
        wpos = past - wb + lax.broadcasted_iota(jnp.int32, (1, wb + PAGE_SIZE), 1)
        dist = tpos - wpos
        gate = gate_ref[...]
        for g in range(g_):
            kk = jnp.concatenate([_slot_rows(win_ref, g, wb), new_rows(new_w_ref, g * hd)], axis=0)
            vv = jnp.concatenate([_slot_rows(win_ref, g_ + g, wb), new_rows(new_w_ref, kvw + g * hd)], axis=0)
            s = _dot_nt(q_ref[g], kk) * scale
            ok = (wpos >= 0) & (dist >= 0) & (dist < NSA_WINDOW)
            e, l = _softmax_rows(jnp.where(ok, s, NEG_INF))
            o_w = _dot(e, vv) / l
            o_s = acc_ref[g] / l_ref[g]
            o_c = oc_ref[g]
            for r in range(reps):
                rs = slice(r * t_new, (r + 1) * t_new)
                c0 = g * LANES + 3 * r
                o = (gate[:, c0:c0 + 1] * o_c[rs] + gate[:, c0 + 1:c0 + 2] * o_s[rs]
                     + gate[:, c0 + 2:c0 + 3] * o_w[rs])
                col = (g * reps + r) * hd
                o_ref[:, col:col + hd] = o.astype(o_ref.dtype)
        keep = (wb - t_new) * KV_SLOTS
        win_out_ref[0:keep, :] = win_ref[t_new * KV_SLOTS:wb * KV_SLOTS, :]
        for slot in range(KV_SLOTS):
            win_out_ref[pl.ds(keep + slot, t_new, stride=KV_SLOTS), :] = new_w_ref[:, slot * hd:(slot + 1) * hd]


def _nsa_sample_attend(q4, bias, o_c, kv_s, win, kv_w, gates, cache, page_table, *, n_pg, past):
    b, g_, rows, hd = q4.shape
    t_new = rows // NSA_REP
    width = n_pg * PAGE_SIZE
    kvw2 = 2 * NSA_KV_WIDTH
    n_pages = page_table.shape[1]
    win_rows = win.shape[1]
    page_rows = cache.shape[1]

    def page_spec(k):
        return pl.BlockSpec((None, page_rows, hd), lambda i, s, pt: (pt[i, s * n_pg + k], 0, 0))

    per_b = lambda shape: pl.BlockSpec((None,) + shape, lambda i, s, pt: (i,) + (0,) * len(shape))
    grid_spec = pltpu.PrefetchScalarGridSpec(
        num_scalar_prefetch=1,
        grid=(b, n_pages // n_pg),
        in_specs=[page_spec(k) for k in range(n_pg)] + [
            per_b((g_, rows, hd)),
            pl.BlockSpec((None, g_, t_new, width), lambda i, s, pt: (i, 0, 0, s)),
            pl.BlockSpec((None, g_, t_new, PAGE_SIZE), lambda i, s, pt: (i, 0, 0, past // PAGE_SIZE)),
            per_b((g_, rows, hd)),
            per_b((t_new, kvw2)), per_b((win_rows, hd)), per_b((t_new, kvw2)),
            per_b((t_new, g_ * LANES))],
        out_specs=[per_b((t_new, NSA_Q_WIDTH)), per_b((win_rows, hd))],
        scratch_shapes=[pltpu.VMEM((g_, rows, 1), F32), pltpu.VMEM((g_, rows, 1), F32),
                        pltpu.VMEM((g_, rows, hd), F32)])
    return pl.pallas_call(
        functools.partial(_nsa_sample_attend_kernel, n_pg=n_pg, t_new=t_new, past=past),
        grid_spec=grid_spec,
        out_shape=[jax.ShapeDtypeStruct((b, t_new, NSA_Q_WIDTH), BF16),
                   jax.ShapeDtypeStruct((b, win_rows, hd), F32)],
        compiler_params=_params("arbitrary", "arbitrary"),
        name="nsa_sample_attention",
    )(page_table, *([cache] * n_pg), q4, bias, bias, o_c, kv_s, win, kv_w, gates)


def _log_sigmoid(x):
    return jnp.minimum(x, 0.0) - jnp.log1p(jnp.exp(-jnp.abs(x)))


def _mlstm_kernel(q_ref, k_ref, v_ref, o_ref, gc_ref, gr_ref, bc_ref, br_ref, ng_ref,
                  c0_ref, n0_ref, m0_ref, y_ref, c_out, n_out, m_out, c_s, n_s, m_s):
    nh, dk, dv = MLSTM_HEADS, MLSTM_QK_DIM, MLSTM_V_DIM
    step = pl.program_id(1)
    nb, ln = q_ref.shape[0], q_ref.shape[1]

    @pl.when(step == 0)
    def _():
        c_s[...] = c0_ref[...]
        n_s[...] = n0_ref[...]
        m_s[...] = m0_ref[...]

    hi = lax.Precision.HIGHEST
    r_i = lax.broadcasted_iota(jnp.int32, (ln, ln), 0)
    c_i = lax.broadcasted_iota(jnp.int32, (ln, ln), 1)
    causal = c_i <= r_i
    lower = jnp.where(causal, 1.0, 0.0)
    upper = jnp.where(r_i <= c_i, 1.0, 0.0)
    ig_c, b_c, ig_r, b_r = [], [], [], []
    for bb in range(nb):
        gcol = gc_ref[bb] + bc_ref[...]
        grow = gr_ref[bb] + br_ref[...]
        ig_c.append(gcol[:, :nh])
        b_c.append(jnp.dot(lower, _log_sigmoid(gcol[:, nh:]), precision=hi, preferred_element_type=F32))
        ig_r.append(grow[:nh, :])
        b_r.append(jnp.dot(_log_sigmoid(grow[nh:, :]), upper, precision=hi, preferred_element_type=F32))

    units = [(bb, h) for bb in range(nb) for h in range(nh)]
    idx = range(len(units))
    q = [q_ref[bb, :, h * dk:(h + 1) * dk] for bb, h in units]
    k = [k_ref[bb, :, h * dk:(h + 1) * dk] * (dk ** -0.5) for bb, h in units]
    vb = [v_ref[bb, :, h * dv:(h + 1) * dv].astype(BF16) for bb, h in units]
    qb = [x.astype(BF16) for x in q]
    c_prev = [c_s[bb, h] for bb, h in units]
    n_prev = [n_s[bb, h] for bb, h in units]
    m_prev = [m_s[bb, h] for bb, h in units]
    bt = [b_c[bb][:, h:h + 1] for bb, h in units]
    b_last = [b_c[bb][ln - 1:ln, h:h + 1] for bb, h in units]
    b_row = [b_r[bb][h:h + 1, :] for bb, h in units]
    i_row = [ig_r[bb][h:h + 1, :] for bb, h in units]
    i_col = [ig_c[bb][:, h:h + 1] for bb, h in units]

    d = [jnp.where(causal, bt[u] - b_row[u] + i_row[u], -jnp.inf) for u in idx]
    inter = [bt[u] + m_prev[u] for u in idx]
    m_t = [jnp.maximum(inter[u], jnp.max(d[u], axis=-1, keepdims=True)) for u in idx]
    a = [jnp.exp(inter[u] - m_t[u]) for u in idx]
    m_new = [m_t[u][ln - 1:ln, :] for u in idx]
    ws = [jnp.exp(b_last[u] - bt[u] + i_col[u] - m_new[u]) for u in idx]
    decay = [jnp.exp(b_last[u] + m_prev[u] - m_new[u]) for u in idx]
    kw = [k[u] * ws[u] for u in idx]

    qk = [_dot_nt(qb[u], k[u]) for u in idx]
    qc = [_dot(qb[u], c_prev[u]) for u in idx]
    kwv = [lax.dot_general(kw[u].astype(BF16), vb[u], _TN, preferred_element_type=F32) for u in idx]
    w = [jnp.exp(d[u] - m_t[u]) * qk[u] for u in idx]
    wv = [_dot(w[u], vb[u]) for u in idx]

    ys = []
    for u, (bb, h) in enumerate(units):
        num = a[u] * qc[u] + wv[u]
        den = a[u] * jnp.sum(q[u] * n_prev[u], axis=-1, keepdims=True) + jnp.sum(w[u], axis=-1, keepdims=True)
        hh = num / jnp.maximum(jnp.abs(den), jnp.exp(-m_t[u]))
        hn = _rms(hh) * ng_ref[:, h * dv:(h + 1) * dv]
        ys.append((hn * _sigmoid(o_ref[bb, :, h * dv:(h + 1) * dv])).astype(y_ref.dtype))

    for u, (bb, h) in enumerate(units):
        y_ref[bb, :, h * dv:(h + 1) * dv] = ys[u]
        c_s[bb, h] = decay[u] * c_prev[u] + kwv[u]
        n_s[bb, h] = decay[u] * n_prev[u] + jnp.sum(kw[u], axis=0, keepdims=True)
        m_s[bb, h] = m_new[u]

    @pl.when(step == pl.num_programs(1) - 1)
    def _():
        c_out[...] = c_s[...]
        n_out[...] = n_s[...]
        m_out[...] = m_s[...]


def _mlstm(proj, graw, b_gate, norm_g, c0, n0, m0, *, chunk, nb):
    b, t, _ = proj.shape
    nh, dk, dv = MLSTM_HEADS, MLSTM_QK_DIM, MLSTM_V_DIM
    nchunk = t // chunk
    qw, vw = nh * dk, nh * dv
    grow = jnp.transpose(graw.reshape(b, nchunk, chunk, 2 * nh), (0, 1, 3, 2))
    state = lambda shape: pl.BlockSpec((nb,) + shape, lambda i, s: (i,) + (0,) * len(shape))
    const = lambda shape: pl.BlockSpec(shape, lambda i, s: (0,) * len(shape))
    return pl.pallas_call(
        _mlstm_kernel,
        grid=(b // nb, nchunk),
        in_specs=[pl.BlockSpec((nb, chunk, qw), lambda i, s: (i, s, 0)),
                  pl.BlockSpec((nb, chunk, qw), lambda i, s: (i, s, 1)),
                  pl.BlockSpec((nb, chunk, vw), lambda i, s: (i, s, 2 * qw // vw)),
                  pl.BlockSpec((nb, chunk, vw), lambda i, s: (i, s, 2 * qw // vw + 1)),
                  pl.BlockSpec((nb, chunk, 2 * nh), lambda i, s: (i, s, 0)),
                  pl.BlockSpec((nb, None, 2 * nh, chunk), lambda i, s: (i, s, 0, 0)),
                  const((1, 2 * nh)), const((2 * nh, 1)), const((1, vw)),
                  state((nh, dk, dv)), state((nh, 1, dk)), state((nh, 1, 1))],
        out_specs=[pl.BlockSpec((nb, chunk, vw), lambda i, s: (i, s, 0)),
                   state((nh, dk, dv)), state((nh, 1, dk)), state((nh, 1, 1))],
        out_shape=[jax.ShapeDtypeStruct((b, t, vw), BF16),
                   jax.ShapeDtypeStruct((b, nh, dk, dv), F32),
                   jax.ShapeDtypeStruct((b, nh, 1, dk), F32),
                   jax.ShapeDtypeStruct((b, nh, 1, 1), F32)],
        scratch_shapes=[pltpu.VMEM((nb, nh, dk, dv), F32), pltpu.VMEM((nb, nh, 1, dk), F32),
                        pltpu.VMEM((nb, nh, 1, 1), F32)],
        compiler_params=_params("arbitrary", "arbitrary"),
        name="mlstm",
    )(proj, proj, proj, proj, graw, grow, b_gate.reshape(1, 2 * nh), b_gate.reshape(2 * nh, 1),
      norm_g.reshape(1, vw), c0, n0.reshape(b, nh, 1, dk), m0.reshape(b, nh, 1, 1))


def _rope_tables(pos):
    half = NSA_HEAD_DIM // 2
    freq = ROPE_THETA ** (-jnp.arange(half, dtype=F32) / half)
    ang = pos.astype(F32)[:, None] * freq[None, :]
    cos, sin = jnp.cos(ang), jnp.sin(ang)
    return jnp.concatenate([cos, cos], axis=-1), jnp.concatenate([-sin, sin], axis=-1)


def _gate_weights(w_in):
    d = w_in.shape[0]
    per_group = NSA_REP * 3
    w = w_in[:, NSA_Q_WIDTH + 6 * NSA_KV_WIDTH:].reshape(d, NSA_KV_GROUPS, per_group)
    w = jnp.pad(w, ((0, 0), (0, 0), (0, LANES - per_group)))
    return w.reshape(d, NSA_KV_GROUPS * LANES)


def _row_tile(m, pref):
    t = min(pref, m)
    while m % t:
        t //= 2
    return t


class _Group:
    def __init__(self, x, mod_rows, kind):
        self.b, self.t, self.d = x.shape
        self.x = x.reshape(self.b * self.t, self.d)
        self.kind = kind
        self.mod_rows = mod_rows

    def mod(self, layer):
        m = self.mod_rows[layer]
        if self.kind == "batch":
            return m.reshape(self.b, 1, m.shape[-1])
        return jnp.repeat(m, self.t, axis=0)


def kernel(x_prompt, x_sample, c_prompt, c_sample, cache_nsa_cmp_kv, cache_nsa_sel_kv, page_table, state_nsa_win_kv, state_mlstm_c, state_mlstm_n, state_mlstm_m, ada_w, ada_b, norm_mix_g, norm_ffn_g, norm_final_g, nsa_w_in, nsa_w_pool, nsa_w_out, mlstm_w_in, mlstm_b_gate, mlstm_norm_g, mlstm_w_out, ffn_w_gate, ffn_w_up, ffn_w_down, moe_w_router, moe_w_gate, moe_w_up, moe_w_down):
    bp, tp, d = x_prompt.shape
    bs, ts, _ = x_sample.shape
    past = page_table.shape[1] * PAGE_SIZE
    assert tp % NSA_BLOCK == 0 and tp >= NSA_WINDOW + ATTN_Q_TILE and NSA_WINDOW % ATTN_Q_TILE == 0 and ts < NSA_BLOCK and ts % SUBLANES == 0
    assert past % NSA_BLOCK == 0 and state_nsa_win_kv.shape[2] == NSA_WINDOW
    assert ada_w.shape[0] == 2 and nsa_w_in.shape[0] == 1 and mlstm_w_in.shape[0] == 1

    mod_all = _adaln(jnp.concatenate([c_prompt, c_sample], axis=0), ada_w, ada_b)
    groups = [_Group(x_prompt, mod_all[:, :bp], "batch"), _Group(x_sample, mod_all[:, bp:], "row")]
    tms = [_row_tile(tp, MM_ROW_TILE), bs * ts]
    tes = [_row_tile(tp, ROW_TILE), bs * ts]

    w_gates = _gate_weights(nsa_w_in[0])
    nsa_w_out_bf16 = nsa_w_out.astype(BF16)
    mlstm_w_out_bf16 = mlstm_w_out.astype(BF16)
    g_, hd = NSA_KV_GROUPS, NSA_HEAD_DIM
    kvw2 = 2 * NSA_KV_WIDTH
    outs = {}
    for gi, (grp, tm, te) in enumerate(zip(groups, tms, tes)):
        b, t = grp.b, grp.t
        mod = grp.mod(0)
        h = _modulate(grp.x, norm_mix_g[0], mod, grp.kind, 0, 1, te, t)
        pos = jnp.arange(t) if gi == 0 else past + jnp.arange(t)
        cos, sin = _rope_tables(pos)
        if gi == 1:
            cos, sin = jnp.tile(cos, (b, 1)), jnp.tile(sin, (b, 1))
        q = _mm_rope(h, nsa_w_in, cos, sin, col0=0, n_out=NSA_Q_WIDTH, rope_cols=NSA_Q_WIDTH,
                     tm=tm, tn=MM_COL_TILE, out_dtype=BF16, name="nsa_q_proj").reshape(b, t, NSA_Q_WIDTH)
        kvp = lambda br, row_major: _kv_proj(h, nsa_w_in, cos, sin, col0=NSA_Q_WIDTH + br * kvw2, tm=te,
                                             row_major=row_major, name="nsa_kv_proj")
        cache_rows = lambda a: a.reshape(b, t * KV_SLOTS, hd)
        (kv_c_n,) = kvp(0, False)
        kv_s_n, kv_s = kvp(1, True)
        kv_w_n, kv_w = kvp(2, True)
        kv_c_n, kv_s_n, kv_w_n = cache_rows(kv_c_n), cache_rows(kv_s_n), cache_rows(kv_w_n)
        kv_s, kv_w = kv_s.reshape(b, t, kvw2), kv_w.reshape(b, t, kvw2)
        gates = _mm(h, w_gates, tm=tm, tn=g_ * LANES, act="sigmoid", name="nsa_gates").reshape(b, t, g_ * LANES)
        if gi == 0:
            ckv = _compress_tokens(kv_c_n, nsa_w_pool[0], _row_tile(t, COMPRESS_TOKENS))
            att = _nsa_prompt_attention(q, ckv, kv_s, kv_w, gates, tq=ATTN_Q_TILE, kc=ATTN_KEY_CHUNK)
            win_out = kv_w_n[:, (t - min(NSA_WINDOW, t)) * KV_SLOTS:]
        else:
            n_phys = cache_nsa_cmp_kv.shape[1]
            native = lambda a, lead: a.reshape(lead, -1, hd)
            ckv = _compress_pages(native(cache_nsa_cmp_kv, n_phys), page_table, nsa_w_pool[0],
                                  _row_tile(page_table.shape[1], PAGES_PER_STEP))
            n_blocks = past // NSA_BLOCK + 1
            n_pad = -(-n_blocks // LANES) * LANES
            q4 = jnp.transpose(q.reshape(b, t, g_, NSA_REP, hd), (0, 2, 3, 1, 4)).reshape(b, g_, NSA_REP * t, hd)
            n_pg = _row_tile(page_table.shape[1], PAGES_PER_STEP)
            o_c, bias = _nsa_sample_select(q4, ckv, t_new=t, past=past, n_blocks=n_blocks, n_pad=n_pad,
                                           n_keys=past + PAGE_SIZE, chunk=n_pg * PAGE_SIZE)
            att, win_out = _nsa_sample_attend(
                q4, bias, o_c, kv_s, native(state_nsa_win_kv, b), kv_w, gates,
                native(cache_nsa_sel_kv, n_phys), page_table, n_pg=n_pg, past=past)
        as_cache = lambda a: a.reshape(1, b, -1, 2, g_, hd)
        outs[gi] = (as_cache(kv_c_n), as_cache(kv_s_n), as_cache(win_out))
        x1, h2 = _mm_res_mod(att.reshape(b * t, NSA_Q_WIDTH), nsa_w_out_bf16, grp.x, mod, grp.kind, 2,
                             norm_ffn_g[0], 3, 4, tm=te, rows_per_batch=t)
        act = _swiglu_up(h2, ffn_w_gate, ffn_w_up, tm=tm, tn=MM_COL_TILE)
        grp.x = _mm_res(act, ffn_w_down, x1, mod, grp.kind, 5, tm=te, tn=MM_COL_TILE, rows_per_batch=t)

    nh, dk, dv = MLSTM_HEADS, MLSTM_QK_DIM, MLSTM_V_DIM
    n_main = 2 * nh * dk + 2 * nh * dv
    w_g = jnp.pad(mlstm_w_in[0][:, n_main:], ((0, 0), (0, LANES - 2 * nh)))
    states = {}
    mods = []
    n_tok = sum(grp.b * grp.t for grp in groups)
    h_all, row0 = None, 0
    w_router = jnp.pad(moe_w_router[0], ((0, 0), (0, LANES - N_EXPERTS)))
    routes = []
    for gi, (grp, tm, te) in enumerate(zip(groups, tms, tes)):
        b, t = grp.b, grp.t
        mod = grp.mod(1)
        h = _modulate(grp.x, norm_mix_g[1], mod, grp.kind, 0, 1, te, t)
        proj = _mm(h, mlstm_w_in, n_out=n_main, tm=tm, tn=MM_WIDE_COL_TILE, name="mlstm_in_proj").reshape(b, t, n_main)
        graw = _mm(h, w_g, tm=tm, tn=LANES, name="mlstm_gates")[:, :2 * nh].reshape(b, t, 2 * nh)
        if gi == 0:
            c0 = jnp.zeros((b, nh, dk, dv), F32)
            n0 = jnp.zeros((b, nh, dk), F32)
            m0 = jnp.zeros((b, nh), F32)
            chunk = _row_tile(t, MLSTM_STEP_TOKENS)
        else:
            c0, n0, m0 = state_mlstm_c[0], state_mlstm_n[0], state_mlstm_m[0]
            chunk = t
        y, c_new, n_new, m_new = _mlstm(proj, graw, mlstm_b_gate[0], mlstm_norm_g[0], c0, n0, m0, chunk=chunk,
                                        nb=1)
        states[gi] = (c_new[None], n_new.reshape(1, b, nh, dk), m_new.reshape(1, b, nh))
        grp.x, h_all, routed = _mm_res_mod(y.reshape(b * t, nh * dv), mlstm_w_out_bf16, grp.x, mod, grp.kind, 2,
                                           norm_ffn_g[1], 3, 4, tm=te, rows_per_batch=t,
                                           into=h_all, row0=row0, total_rows=n_tok, w_router=w_router)
        routes.append(routed[:, :2 * TOP_K])
        row0 += b * t
        mods.append(mod)

    tm_moe = MOE_ROW_BLOCK
    routes = jnp.concatenate(routes, axis=0)
    wts, top_e = routes[:, :TOP_K], routes[:, TOP_K:].astype(jnp.int32)
    n_asg = n_tok * TOP_K
    flat_e = top_e.reshape(n_asg)
    onehot = (flat_e[:, None] == jnp.arange(N_EXPERTS)[None, :]).astype(jnp.int32)
    within = jnp.take_along_axis(jnp.cumsum(onehot, axis=0), flat_e[:, None], axis=1)[:, 0] - 1
    counts = jnp.sum(onehot, axis=0)
    padded = (counts + tm_moe - 1) // tm_moe * tm_moe
    p_end = jnp.cumsum(padded)
    dest = (p_end - padded)[flat_e] + within
    n_blk = -(-(n_asg + N_EXPERTS * (tm_moe - 1)) // tm_moe)
    n_rows = n_blk * tm_moe
    row_tok = jnp.zeros((n_rows,), jnp.int32).at[dest].set(jnp.arange(n_asg, dtype=jnp.int32) // TOP_K)
    blk_start = jnp.arange(n_blk, dtype=jnp.int32) * tm_moe
    blk_e = jnp.minimum(jnp.sum((p_end[None, :] <= blk_start[:, None]).astype(jnp.int32), axis=1), N_EXPERTS - 1)
    meta = jnp.concatenate([blk_e, p_end[-1:] // tm_moe]).astype(jnp.int32)
    xb = h_all[row_tok]
    act = _moe_up(xb, meta, moe_w_gate, moe_w_up, tm=tm_moe, tn=MM_WIDE_COL_TILE)
    yb = _moe_down(act, meta, moe_w_down, tm=tm_moe, tn=MM_COL_TILE)
    dest2 = dest.reshape(n_tok, TOP_K)

    finals = []
    off = 0
    for gi, (grp, te) in enumerate(zip(groups, tes)):
        m = grp.b * grp.t
        dg = dest2[off:off + m]
        fin = _combine_norm(grp.x, yb[dg[:, 0]], yb[dg[:, 1]], wts[off:off + m], mods[gi], grp.kind, 5,
                            norm_final_g, tm=te, rows_per_batch=grp.t)
        finals.append(fin.reshape(grp.b, grp.t, d))
        off += m

    return (finals[0], finals[1],
            outs[0][0], outs[0][1], outs[0][2], states[0][0], states[0][1], states[0][2],
            outs[1][0], outs[1][1], outs[1][2], states[1][0], states[1][1], states[1][2])
```
